```python
import jax, jax.numpy as jnp
from jax import lax
import numpy as np

D_MODEL = 1024
BATCH = 8
SEQ = 4096
DEPTH = 4

CHUNK = 64
HEAD_DIM = 64
N_HEADS_A = 8
N_HEADS_B = 8
WIDTH_A = N_HEADS_A * HEAD_DIM
WIDTH_B = N_HEADS_B * HEAD_DIM
LEFT_CHUNKS = 8
BAND = (LEFT_CHUNKS + 1) * CHUNK
MAX_REL = 128
N_REL = 2 * MAX_REL + 1
SB_BLOCK = 128
N_EXPERTS = 16
N_GROUPS = 4
EXPERTS_PER_GROUP = N_EXPERTS // N_GROUPS
TOP_K = 2
D_EXPERT = 512
ALPHA = (2 * DEPTH) ** 0.25
BETA_INIT = (8 * DEPTH) ** -0.25
LN_EPS = 1e-5
SPLIT_A = 3 * WIDTH_A
SPLIT_B = SPLIT_A + 3 * WIDTH_B
PROJ_COLS = SPLIT_B + 2 * D_MODEL

kernel_name = "hybrid_chunked_relpos_stickbreaking_grouped_moe_deepnorm"


def layer_norm(x, g, b):
    xf = x.astype(jnp.float32)
    mu = jnp.mean(xf, axis=-1, keepdims=True)
    var = jnp.mean(jnp.square(xf - mu), axis=-1, keepdims=True)
    return ((xf - mu) * lax.rsqrt(var + LN_EPS)).astype(x.dtype) * g + b


def chunked_relpos_attention(q, k, v, rel_bias):
    b, h, s, dh = q.shape
    n_chunks = s // CHUNK
    pad = LEFT_CHUNKS * CHUNK
    k_pad = jnp.pad(k, ((0, 0), (0, 0), (pad, 0), (0, 0)))
    v_pad = jnp.pad(v, ((0, 0), (0, 0), (pad, 0), (0, 0)))
    q_chunks = q.reshape(b, h, n_chunks, CHUNK, dh).transpose(2, 0, 1, 3, 4)
    qi = jnp.arange(CHUNK)[:, None]
    kj = jnp.arange(BAND)[None, :]
    dist = pad + qi - kj
    rel_idx = jnp.clip(dist, -MAX_REL, MAX_REL) + MAX_REL
    bias = rel_bias[:, rel_idx].astype(jnp.float32)
    scale = dh ** -0.5

    def one_chunk(args):
        c, qc = args
        start = c * CHUNK
        kb = lax.dynamic_slice_in_dim(k_pad, start, BAND, axis=2)
        vb = lax.dynamic_slice_in_dim(v_pad, start, BAND, axis=2)
        scores = jnp.einsum('bhqd,bhkd->bhqk', qc, kb).astype(jnp.float32) * scale + bias
        valid = (kj + start - pad) >= 0
        scores = jnp.where(valid[None, None], scores, -jnp.inf)
        p = jax.nn.softmax(scores, axis=-1).astype(v.dtype)
        return jnp.einsum('bhqk,bhkd->bhqd', p, vb)

    out = lax.map(one_chunk, (jnp.arange(n_chunks), q_chunks))
    return out.transpose(1, 2, 0, 3, 4).reshape(b, h, s, dh)


def stick_breaking_attention(q, k, v):
    b, h, s, dh = q.shape
    n_blocks = s // SB_BLOCK
    scale = dh ** -0.5
    q_blocks = q.reshape(b, h, n_blocks, SB_BLOCK, dh).transpose(2, 0, 1, 3, 4)
    key_pos = jnp.arange(s)

    def one_block(args):
        blk, qb = args
        q_pos = blk * SB_BLOCK + jnp.arange(SB_BLOCK)
        z = jnp.einsum('bhqd,bhkd->bhqk', qb, k).astype(jnp.float32) * scale
        before = (key_pos[None, :] < q_pos[:, None])[None, None]
        log_beta = jax.nn.log_sigmoid(z)
        log_keep = jnp.where(before, jax.nn.log_sigmoid(-z), 0.0)
        suffix = lax.cumsum(log_keep, axis=3, reverse=True) - log_keep
        a = jnp.where(before, jnp.exp(log_beta + suffix), 0.0).astype(v.dtype)
        return jnp.einsum('bhqk,bhkd->bhqd', a, v)

    out = lax.map(one_block, (jnp.arange(n_blocks), q_blocks))
    return out.transpose(1, 2, 0, 3, 4).reshape(b, h, s, dh)


def hybrid_mixer(x, w_in, rel_bias, w_up_a, w_up_b, w_out):
    b, s, _ = x.shape
    proj = x @ w_in
    qkv_a, qkv_b, gate_logits = jnp.split(proj, [SPLIT_A, SPLIT_B], axis=-1)

    def to_heads(t, n):
        return t.reshape(b, s, 3, n, HEAD_DIM).transpose(2, 0, 3, 1, 4)

    qa, ka, va = to_heads(qkv_a, N_HEADS_A)
    qb, kb, vb = to_heads(qkv_b, N_HEADS_B)
    ya = chunked_relpos_attention(qa, ka, va, rel_bias)
    yb = stick_breaking_attention(qb, kb, vb)
    ya = ya.transpose(0, 2, 1, 3).reshape(b, s, WIDTH_A) @ w_up_a
    yb = yb.transpose(0, 2, 1, 3).reshape(b, s, WIDTH_B) @ w_up_b
    gate_a, gate_b = jnp.split(jax.nn.sigmoid(gate_logits), 2, axis=-1)
    return (gate_a * ya + gate_b * yb) @ w_out


def grouped_top2_moe(x, w_router, b_router, w_gate, w_up, w_down):
    b, s, d = x.shape
    x2d = x.reshape(b * s, d)
    logits = x2d.astype(jnp.float32) @ w_router.astype(jnp.float32) + b_router.astype(jnp.float32)
    probs = jax.nn.softmax(logits, axis=-1)
    grouped = probs.reshape(-1, N_GROUPS, EXPERTS_PER_GROUP)
    group_idx = jnp.argmax(jnp.max(grouped, axis=-1), axis=-1)
    in_group = jnp.take_along_axis(grouped, group_idx[:, None, None], axis=1)[:, 0]
    top_w, top_i = lax.top_k(in_group, TOP_K)
    top_w = top_w / jnp.sum(top_w, axis=-1, keepdims=True)
    expert_id = group_idx[:, None] * EXPERTS_PER_GROUP + top_i
    combine = jnp.sum(jax.nn.one_hot(expert_id, N_EXPERTS, dtype=jnp.float32) * top_w[..., None], axis=1)
    combine = combine.astype(x.dtype)
    out = jnp.zeros_like(x2d)
    for e in range(N_EXPERTS):
        hdn = jax.nn.silu(x2d @ w_gate[e]) * (x2d @ w_up[e])
        out = out + combine[:, e:e + 1] * (hdn @ w_down[e])
    return out.reshape(b, s, d)


def setup_inputs(seed: int = 0) -> dict:
    key = jax.random.key(seed)
    ks = jax.random.split(key, 16)
    f32 = jnp.float32
    col_scale = np.ones((PROJ_COLS,), np.float32)
    col_scale[2 * WIDTH_A:3 * WIDTH_A] = BETA_INIT
    col_scale[SPLIT_A + 2 * WIDTH_B:SPLIT_B] = BETA_INIT
    x = jax.random.normal(ks[0], (BATCH, SEQ, D_MODEL), f32)
    w_in = jax.random.normal(ks[1], (DEPTH, D_MODEL, PROJ_COLS), f32) * D_MODEL ** -0.5 * jnp.asarray(col_scale)
    rel_bias = jax.random.normal(ks[2], (DEPTH, N_HEADS_A, N_REL), f32) * 0.1
    w_up_a = jax.random.normal(ks[3], (DEPTH, WIDTH_A, D_MODEL), f32) * WIDTH_A ** -0.5 * BETA_INIT
    w_up_b = jax.random.normal(ks[4], (DEPTH, WIDTH_B, D_MODEL), f32) * WIDTH_B ** -0.5 * BETA_INIT
    w_out = jax.random.normal(ks[5], (DEPTH, D_MODEL, D_MODEL), f32) * D_MODEL ** -0.5 * BETA_INIT
    ln1_g = 1.0 + 0.02 * jax.random.normal(ks[6], (DEPTH, D_MODEL), f32)
    ln1_b = 0.02 * jax.random.normal(ks[7], (DEPTH, D_MODEL), f32)
    w_router = jax.random.normal(ks[8], (D_MODEL, N_EXPERTS), f32) * D_MODEL ** -0.5
    b_router = 0.01 * jax.random.normal(ks[9], (N_EXPERTS,), f32)
    w_gate = jax.random.normal(ks[10], (DEPTH, N_EXPERTS, D_MODEL, D_EXPERT), f32) * D_MODEL ** -0.5 * BETA_INIT
    w_up = jax.random.normal(ks[11], (DEPTH, N_EXPERTS, D_MODEL, D_EXPERT), f32) * D_MODEL ** -0.5 * BETA_INIT
    w_down = jax.random.normal(ks[12], (DEPTH, N_EXPERTS, D_EXPERT, D_MODEL), f32) * D_EXPERT ** -0.5 * BETA_INIT
    ln2_g = 1.0 + 0.02 * jax.random.normal(ks[13], (DEPTH, D_MODEL), f32)
    ln2_b = 0.02 * jax.random.normal(ks[14], (DEPTH, D_MODEL), f32)
    return {"x": x, "w_in": w_in, "rel_bias": rel_bias, "w_up_a": w_up_a, "w_up_b": w_up_b,
            "w_out": w_out, "ln1_g": ln1_g, "ln1_b": ln1_b, "w_router": w_router,
            "b_router": b_router, "w_gate": w_gate, "w_up": w_up, "w_down": w_down,
            "ln2_g": ln2_g, "ln2_b": ln2_b}


def reference(x, w_in, rel_bias, w_up_a, w_up_b, w_out, ln1_g, ln1_b, w_router,
              b_router, w_gate, w_up, w_down, ln2_g, ln2_b):
    for l in range(DEPTH):
        mixed = hybrid_mixer(x, w_in[l], rel_bias[l], w_up_a[l], w_up_b[l], w_out[l])
        x = layer_norm(ALPHA * x + mixed, ln1_g[l], ln1_b[l])
        ffn = grouped_top2_moe(x, w_router, b_router, w_gate[l], w_up[l], w_down[l])
        x = layer_norm(ALPHA * x + ffn, ln2_g[l], ln2_b[l])
    return x
```

```python
import functools

import numpy as np
import jax
import jax.numpy as jnp
from jax import lax
from jax.experimental import pallas as pl
from jax.experimental.pallas import tpu as pltpu

F32 = jnp.float32
BF16 = jnp.bfloat16

D_MODEL = 1024
HEAD_DIM = 64
N_HEADS = 8
WIDTH = N_HEADS * HEAD_DIM
CHUNK = 64
LEFT_CHUNKS = 8
MAX_REL = 128
N_EXPERTS = 16
N_GROUPS = 4
EXPERTS_PER_GROUP = 4
D_EXPERT = 512
LN_EPS = 1e-5
PROJ_COLS = 6 * WIDTH + 2 * D_MODEL
N_PAIRS = EXPERTS_PER_GROUP * (EXPERTS_PER_GROUP - 1) // 2
N_CLASSES = N_GROUPS * N_PAIRS

LANES = 128
HEADS_PER_LANE_BLOCK = LANES // HEAD_DIM
N_HEAD_PAIRS = N_HEADS // HEADS_PER_LANE_BLOCK
VMEM_LIMIT = 56 * 1024 * 1024

PROJ_TM = 512
PROJ_TN = 512
ATT_TQ = 256
ATT_WIN = ATT_TQ + LEFT_CHUNKS * CHUNK
SB_BLK = 128
SB_EXIT = -110.0
MIX_TM = 256
MOE_TE = 256
META_W = LANES

NEG_BIG = -1e30


def _cparams(sem):
    return pltpu.CompilerParams(dimension_semantics=sem, vmem_limit_bytes=VMEM_LIMIT)


def _inproj_kernel(x_ref, w_ref, o_ref, *, scale):
    xb = x_ref[...].astype(BF16)
    n_chunks = PROJ_COLS // PROJ_TN
    q_chunks = (0, 3 * WIDTH // PROJ_TN)
    gate_start = 6 * WIDTH // PROJ_TN
    for c in range(n_chunks):
        cols = slice(c * PROJ_TN, (c + 1) * PROJ_TN)
        acc = jnp.dot(xb, w_ref[:, cols], preferred_element_type=F32)
        if c in q_chunks:
            acc = acc * scale
        elif c >= gate_start:
            acc = jax.nn.sigmoid(acc)
        o_ref[:, cols] = acc.astype(BF16)


def _inproj(x2d, w_bf16):
    t = x2d.shape[0]
    assert PROJ_TN == WIDTH
    return pl.pallas_call(
        functools.partial(_inproj_kernel, scale=HEAD_DIM ** -0.5),
        grid=(t // PROJ_TM,),
        in_specs=[pl.BlockSpec((PROJ_TM, D_MODEL), lambda i: (i, 0)),
                  pl.BlockSpec((D_MODEL, PROJ_COLS), lambda i: (0, 0))],
        out_specs=pl.BlockSpec((PROJ_TM, PROJ_COLS), lambda i: (i, 0)),
        out_shape=jax.ShapeDtypeStruct((t, PROJ_COLS), BF16),
        compiler_params=_cparams(("parallel",)),
    )(x2d, w_bf16)


def _attn_bias_tiles(rel_bias):
    i = np.arange(ATT_TQ)[:, None]
    j = np.arange(ATT_WIN)[None, :]
    tiles = []
    for case in range(3):
        qpos = case * ATT_TQ + i
        dist = qpos - j
        qc, kc = qpos // CHUNK, j // CHUNK
        allowed = (kc <= qc) & (kc >= qc - LEFT_CHUNKS)
        idx = np.clip(dist, -MAX_REL, MAX_REL) + MAX_REL
        b = rel_bias[:, idx].astype(F32)
        tiles.append(jnp.where(jnp.asarray(allowed)[None], b, NEG_BIG))
    return jnp.stack(tiles)


def _attn_a_kernel(q_ref, k_ref, v_ref, bias_ref, o_ref):
    qt = pl.program_id(2)
    start = pl.multiple_of(jnp.maximum(qt * ATT_TQ - LEFT_CHUNKS * CHUNK, 0), ATT_TQ)
    q = q_ref[...]
    kw = k_ref[pl.ds(start, ATT_WIN), :]
    vw = v_ref[pl.ds(start, ATT_WIN), :]
    first_head = lax.broadcasted_iota(jnp.int32, (1, LANES), 1) < HEAD_DIM
    outs = []
    for h in range(HEADS_PER_LANE_BLOCK):
        head_lanes = first_head if h == 0 else jnp.logical_not(first_head)
        qh = jnp.where(head_lanes, q, jnp.zeros_like(q))
        s = lax.dot_general(qh, kw, (((1,), (1,)), ((), ())), preferred_element_type=F32)
        s = s + bias_ref[0, h]
        m = jnp.max(s, axis=-1, keepdims=True)
        p = jnp.exp(s - m)
        l = jnp.sum(p, axis=-1, keepdims=True)
        o = jnp.dot(p.astype(BF16), vw, preferred_element_type=F32)
        outs.append(o / l)
    o_ref[...] = jnp.where(first_head, outs[0], outs[1]).astype(BF16)


def _attn_a(proj, bias_tiles, batch, seq):
    n_qt = seq // ATT_TQ
    q_col, k_col, v_col = 0, WIDTH // LANES, 2 * WIDTH // LANES
    last_case = bias_tiles.shape[0] - 1
    return pl.pallas_call(
        _attn_a_kernel,
        grid=(batch, N_HEAD_PAIRS, n_qt),
        in_specs=[
            pl.BlockSpec((ATT_TQ, LANES), lambda b, hp, qt: (b * n_qt + qt, q_col + hp)),
            pl.BlockSpec((seq, LANES), lambda b, hp, qt: (b, k_col + hp)),
            pl.BlockSpec((seq, LANES), lambda b, hp, qt: (b, v_col + hp)),
            pl.BlockSpec((1, HEADS_PER_LANE_BLOCK, ATT_TQ, ATT_WIN),
                         lambda b, hp, qt: (jnp.minimum(qt, last_case), hp, 0, 0)),
        ],
        out_specs=pl.BlockSpec((ATT_TQ, LANES), lambda b, hp, qt: (b * n_qt + qt, hp)),
        out_shape=jax.ShapeDtypeStruct((batch * seq, WIDTH), BF16),
        compiler_params=_cparams(("parallel", "parallel", "arbitrary")),
    )(proj, proj, proj, bias_tiles)


def _sb_scan_matrix():
    jp = np.arange(SB_BLK)[:, None]
    c = np.arange(2 * SB_BLK)[None, :]
    half = np.where(c < SB_BLK, jp > c, True)
    return jnp.asarray(np.concatenate([half, half], axis=0), dtype=BF16)


def _sb_kernel(q_ref, k_ref, v_ref, m_ref, o_ref, acc_ref, carry_ref):
    qb = pl.program_id(1)
    first_head = lax.broadcasted_iota(jnp.int32, (1, LANES), 1) < HEAD_DIM
    row = lax.broadcasted_iota(jnp.int32, (SB_BLK, 2 * SB_BLK), 0)
    col = lax.broadcasted_iota(jnp.int32, (SB_BLK, 2 * SB_BLK), 1)
    before2 = (col & (SB_BLK - 1)) < row

    acc_ref[...] = jnp.zeros_like(acc_ref)
    carry_ref[...] = jnp.zeros_like(carry_ref)

    def block(kb, diag):
        koff = pl.multiple_of(kb * SB_BLK, SB_BLK)
        worst = None
        for hp in range(N_HEAD_PAIRS):
            lanes = slice(hp * LANES, (hp + 1) * LANES)
            q = q_ref[:, lanes]
            k = k_ref[pl.ds(koff, SB_BLK), lanes]
            v = v_ref[pl.ds(koff, SB_BLK), lanes]
            zero = jnp.zeros_like(k)
            k2 = jnp.concatenate([jnp.where(first_head, k, zero), jnp.where(first_head, zero, k)], axis=0)
            v2 = jnp.concatenate([jnp.where(first_head, v, zero), jnp.where(first_head, zero, v)], axis=0)
            z = lax.dot_general(q, k2, (((1,), (1,)), ((), ())), preferred_element_type=F32)
            t = jnp.log(1.0 + jnp.exp(-jnp.abs(z)))
            log_beta = jnp.minimum(z, 0.0) - t
            log_keep = jnp.minimum(-z, 0.0) - t
            if diag:
                log_keep = jnp.where(before2, log_keep, 0.0)
            a_heads = []
            for h in range(HEADS_PER_LANE_BLOCK):
                hl = slice(h * SB_BLK, (h + 1) * SB_BLK)
                lk = log_keep[:, hl]
                hi = lk.astype(BF16)
                lo = (lk - hi.astype(F32)).astype(BF16)
                st = jnp.dot(jnp.concatenate([hi, lo], axis=1), m_ref[...], preferred_element_type=F32)
                idx = hp * HEADS_PER_LANE_BLOCK + h
                carry = carry_ref[idx]
                a_heads.append(jnp.exp(log_beta[:, hl] + st[:, :SB_BLK] + carry))
                carry = carry + st[:, SB_BLK:]
                carry_ref[idx] = carry
                worst = carry if worst is None else jnp.maximum(worst, carry)
            a = jnp.concatenate(a_heads, axis=1)
            if diag:
                a = jnp.where(before2, a, 0.0)
            acc_ref[hp] += jnp.dot(a.astype(BF16), v2, preferred_element_type=F32)
        return jnp.max(worst)

    worst0 = block(qb, True)

    def cond(state):
        kb, worst = state
        return jnp.logical_and(kb >= 0, worst >= SB_EXIT)

    def body(state):
        kb, _ = state
        return kb - 1, block(kb, False)

    lax.while_loop(cond, body, (qb - 1, worst0))

    for hp in range(N_HEAD_PAIRS):
        o_ref[:, hp * LANES:(hp + 1) * LANES] = acc_ref[hp].astype(BF16)


def _attn_b(proj, scan_m, batch, seq):
    n_qb = seq // SB_BLK
    q_col, k_col, v_col = 3, 4, 5
    return pl.pallas_call(
        _sb_kernel,
        grid=(batch, n_qb),
        in_specs=[
            pl.BlockSpec((SB_BLK, WIDTH), lambda b, qb: (b * n_qb + qb, q_col)),
            pl.BlockSpec((seq, WIDTH), lambda b, qb: (b, k_col)),
            pl.BlockSpec((seq, WIDTH), lambda b, qb: (b, v_col)),
            pl.BlockSpec((2 * SB_BLK, 2 * SB_BLK), lambda b, qb: (0, 0)),
        ],
        out_specs=pl.BlockSpec((SB_BLK, WIDTH), lambda b, qb: (b * n_qb + qb, 0)),
        out_shape=jax.ShapeDtypeStruct((batch * seq, WIDTH), BF16),
        scratch_shapes=[pltpu.VMEM((N_HEAD_PAIRS, SB_BLK, LANES), F32),
                        pltpu.VMEM((N_HEADS, SB_BLK, SB_BLK), F32)],
        compiler_params=_cparams(("parallel", "arbitrary")),
    )(proj, proj, proj, scan_m)


def _layer_norm(h, g, b):
    mu = jnp.mean(h, axis=-1, keepdims=True)
    d = h - mu
    var = jnp.mean(d * d, axis=-1, keepdims=True)
    return d * lax.rsqrt(var + LN_EPS) * g + b


def _split_bf16(x):
    hi = x.astype(BF16)
    return hi, (x - hi.astype(F32)).astype(BF16)


def _mix_kernel(ya_ref, yb_ref, ga_ref, gb_ref, x_ref, wa_ref, wb_ref, wo_ref, g_ref, b_ref,
                wrh_ref, wrl_ref, br_ref, tri_ref, o_ref, cnt_ref, run_ref, *, alpha):
    @pl.when(pl.program_id(0) == 0)
    def _():
        run_ref[...] = jnp.zeros_like(run_ref)

    pa = jnp.dot(ya_ref[...], wa_ref[...], preferred_element_type=F32)
    pb = jnp.dot(yb_ref[...], wb_ref[...], preferred_element_type=F32)
    merged = ga_ref[...].astype(F32) * pa + gb_ref[...].astype(F32) * pb
    mixed = jnp.dot(merged.astype(BF16), wo_ref[...], preferred_element_type=F32)
    x1 = _layer_norm(alpha * x_ref[...] + mixed, g_ref[...], b_ref[...])
    o_ref[:, :D_MODEL] = x1

    xh, xl = _split_bf16(x1)
    logits = (jnp.dot(xh, wrh_ref[...], preferred_element_type=F32)
              + jnp.dot(xl, wrh_ref[...], preferred_element_type=F32)
              + jnp.dot(xh, wrl_ref[...], preferred_element_type=F32)
              + jnp.dot(xl, wrl_ref[...], preferred_element_type=F32)) + br_ref[...]
    lane = lax.broadcasted_iota(jnp.int32, logits.shape, 1)
    logits = jnp.where(lane < N_EXPERTS, logits, NEG_BIG)
    e = jnp.exp(logits - jnp.max(logits, axis=-1, keepdims=True))
    probs = e / jnp.sum(e, axis=-1, keepdims=True)
    p1 = jnp.max(probs, axis=-1, keepdims=True)
    i1 = jnp.min(jnp.where(probs == p1, lane, LANES), axis=-1, keepdims=True)
    group = i1 >> 2
    rest = jnp.where(jnp.logical_and(lane >> 2 == group, lane != i1), probs, -1.0)
    p2 = jnp.max(rest, axis=-1, keepdims=True)
    i2 = jnp.min(jnp.where(rest == p2, lane, LANES), axis=-1, keepdims=True)
    w1 = p1 / (p1 + p2)
    w2 = p2 / (p1 + p2)
    first_low = i1 < i2
    e_lo = jnp.where(first_low, i1, i2) & 3
    e_hi = jnp.where(first_low, i2, i1) & 3
    w_lo = jnp.where(first_low, w1, w2)
    w_hi = jnp.where(first_low, w2, w1)
    cls = group * N_PAIRS + ((e_lo * (5 - e_lo)) >> 1) + e_hi - 1

    onehot = (lane == cls).astype(F32)
    within = jnp.dot(tri_ref[...], onehot.astype(BF16), preferred_element_type=F32)
    rank = jnp.sum((within + run_ref[...]) * onehot, axis=-1, keepdims=True)
    run_ref[...] = run_ref[...] + jnp.sum(onehot, axis=0, keepdims=True)
    cnt_ref[...] = jnp.broadcast_to(run_ref[...], cnt_ref.shape)

    meta = jnp.where(lane == 0, cls.astype(F32),
                     jnp.where(lane == 1, rank,
                               jnp.where(lane == 2, w_lo, jnp.where(lane == 3, w_hi, 0.0))))
    o_ref[:, D_MODEL:] = meta


def _mix(ya, yb, proj, x2d, wa, wb, wo, g, b, wrh, wrl, br, tri, alpha):
    t = x2d.shape[0]
    ga_col = 6 * WIDTH // D_MODEL
    const = lambda shape: pl.BlockSpec(shape, lambda i: (0,) * len(shape))
    return pl.pallas_call(
        functools.partial(_mix_kernel, alpha=alpha),
        grid=(t // MIX_TM,),
        in_specs=[
            pl.BlockSpec((MIX_TM, WIDTH), lambda i: (i, 0)),
            pl.BlockSpec((MIX_TM, WIDTH), lambda i: (i, 0)),
            pl.BlockSpec((MIX_TM, D_MODEL), lambda i: (i, ga_col)),
            pl.BlockSpec((MIX_TM, D_MODEL), lambda i: (i, ga_col + 1)),
            pl.BlockSpec((MIX_TM, D_MODEL), lambda i: (i, 0)),
            const((WIDTH, D_MODEL)), const((WIDTH, D_MODEL)), const((D_MODEL, D_MODEL)),
            const((1, D_MODEL)), const((1, D_MODEL)),
            const((D_MODEL, LANES)), const((D_MODEL, LANES)), const((1, LANES)),
            const((MIX_TM, MIX_TM)),
        ],
        out_specs=[pl.BlockSpec((MIX_TM, D_MODEL + META_W), lambda i: (i, 0)),
                   pl.BlockSpec((8, LANES), lambda i: (0, 0))],
        out_shape=[jax.ShapeDtypeStruct((t, D_MODEL + META_W), F32),
                   jax.ShapeDtypeStruct((8, LANES), F32)],
        scratch_shapes=[pltpu.VMEM((1, LANES), F32)],
        compiler_params=_cparams(("arbitrary",)),
    )(ya, yb, proj, proj, x2d, wa, wb, wo, g, b, wrh, wrl, br, tri)


def _moe_kernel(tlo_ref, thi_ref, nvalid_ref, ntiles_ref,
                src_ref, x_hbm, wg0_ref, wu0_ref, wd0_ref, wg1_ref, wu1_ref, wd1_ref, g_ref, b_ref,
                out_hbm, xg_ref, y_ref, sem_in, sem_out, *, alpha):
    i = pl.program_id(0)

    def gather_copy(r):
        return pltpu.make_async_copy(x_hbm.at[pl.ds(src_ref[0, 0, r], 1)], xg_ref.at[pl.ds(r, 1)], sem_in)

    def scatter_copy(r):
        return pltpu.make_async_copy(y_ref.at[pl.ds(r, 1)], out_hbm.at[pl.ds(src_ref[0, 0, r], 1)], sem_out)

    @pl.when(i < ntiles_ref[0])
    def _():
        def start_gather(r, c):
            gather_copy(r).start()
            return c

        def wait_gather(r, c):
            gather_copy(r).wait()
            return c

        lax.fori_loop(0, MOE_TE, start_gather, 0)
        lax.fori_loop(0, MOE_TE, wait_gather, 0)

        xg = xg_ref[...]
        x = xg[:, :D_MODEL]
        xb = x.astype(BF16)
        w_lo = xg[:, D_MODEL + 2:D_MODEL + 3]
        w_hi = xg[:, D_MODEL + 3:D_MODEL + 4]

        def expert(wg_ref, wu_ref, wd_ref):
            gate = jnp.dot(xb, wg_ref[0], preferred_element_type=F32)
            up = jnp.dot(xb, wu_ref[0], preferred_element_type=F32)
            hidden = (gate * jax.nn.sigmoid(gate) * up).astype(BF16)
            return jnp.dot(hidden, wd_ref[0], preferred_element_type=F32)

        ffn = w_lo * expert(wg0_ref, wu0_ref, wd0_ref) + w_hi * expert(wg1_ref, wu1_ref, wd1_ref)
        y_ref[...] = _layer_norm(alpha * x + ffn, g_ref[...], b_ref[...])

        n = nvalid_ref[i]

        def start_scatter(r, c):
            scatter_copy(r).start()
            return c

        def wait_scatter(r, c):
            scatter_copy(r).wait()
            return c

        lax.fori_loop(0, n, start_scatter, 0)
        lax.fori_loop(0, n, wait_scatter, 0)


def _moe(x1ext, tile_lo, tile_hi, tile_nvalid, ntiles, src, wg, wu, wd, g, b, alpha):
    t = x1ext.shape[0]
    n_tiles_max = src.shape[0]
    wspec_in = lambda which: pl.BlockSpec(
        (1, D_MODEL, D_EXPERT), lambda i, tlo, thi, nv, nt: ((tlo, thi)[which][i], 0, 0))
    wspec_out = lambda which: pl.BlockSpec(
        (1, D_EXPERT, D_MODEL), lambda i, tlo, thi, nv, nt: ((tlo, thi)[which][i], 0, 0))
    vec = pl.BlockSpec((1, D_MODEL), lambda i, *_: (0, 0))
    grid_spec = pltpu.PrefetchScalarGridSpec(
        num_scalar_prefetch=4,
        grid=(n_tiles_max,),
        in_specs=[
            pl.BlockSpec((1, 1, MOE_TE), lambda i, *_: (i, 0, 0), memory_space=pltpu.SMEM),
            pl.BlockSpec(memory_space=pl.ANY),
            wspec_in(0), wspec_in(0), wspec_out(0), wspec_in(1), wspec_in(1), wspec_out(1),
            vec, vec,
        ],
        out_specs=pl.BlockSpec(memory_space=pl.ANY),
        scratch_shapes=[pltpu.VMEM((MOE_TE, D_MODEL + META_W), F32),
                        pltpu.VMEM((MOE_TE, D_MODEL), F32),
                        pltpu.SemaphoreType.DMA(()), pltpu.SemaphoreType.DMA(())],
    )
    return pl.pallas_call(
        functools.partial(_moe_kernel, alpha=alpha),
        grid_spec=grid_spec,
        out_shape=jax.ShapeDtypeStruct((t, D_MODEL), F32),
        compiler_params=_cparams(("arbitrary",)),
    )(tile_lo, tile_hi, tile_nvalid, ntiles, src, x1ext, wg, wu, wd, wg, wu, wd, g, b)


def _class_experts():
    lo, hi = [], []
    for grp in range(N_GROUPS):
        for a in range(EXPERTS_PER_GROUP):
            for c in range(a + 1, EXPERTS_PER_GROUP):
                lo.append(grp * EXPERTS_PER_GROUP + a)
                hi.append(grp * EXPERTS_PER_GROUP + c)
    return np.asarray(lo, np.int32), np.asarray(hi, np.int32)


def _routing_tables(x1ext, counts, n_tiles_max):
    t = x1ext.shape[0]
    cls = x1ext[:, D_MODEL].astype(jnp.int32)
    rank = x1ext[:, D_MODEL + 1].astype(jnp.int32)
    cnt = counts[0, :N_CLASSES].astype(jnp.int32)
    padded = ((cnt + MOE_TE - 1) // MOE_TE) * MOE_TE
    ends = jnp.cumsum(padded)
    offs = ends - padded
    dest = offs[cls] + rank
    src = jnp.zeros((n_tiles_max * MOE_TE,), jnp.int32).at[dest].set(jnp.arange(t, dtype=jnp.int32))
    tile_start = jnp.arange(n_tiles_max, dtype=jnp.int32) * MOE_TE
    tile_cls = jnp.minimum(jnp.searchsorted(ends, tile_start, side="right"), N_CLASSES - 1).astype(jnp.int32)
    nvalid = jnp.clip(offs[tile_cls] + cnt[tile_cls] - tile_start, 0, MOE_TE).astype(jnp.int32)
    ntiles = (ends[-1] // MOE_TE).astype(jnp.int32).reshape(1)
    e_lo, e_hi = _class_experts()
    return (jnp.asarray(e_lo)[tile_cls], jnp.asarray(e_hi)[tile_cls], nvalid, ntiles,
            src.reshape(n_tiles_max, 1, MOE_TE))


def kernel(x, w_in, rel_bias, w_up_a, w_up_b, w_out, ln1_g, ln1_b, w_router, b_router,
           w_gate, w_up, w_down, ln2_g, ln2_b):
    batch, seq, d = x.shape
    depth = w_in.shape[0]
    assert d == D_MODEL and seq % ATT_TQ == 0 and seq >= ATT_WIN
    t = batch * seq
    assert t % PROJ_TM == 0 and t % MIX_TM == 0
    alpha = (2 * depth) ** 0.25
    n_tiles_max = t // MOE_TE + N_CLASSES

    scan_m = _sb_scan_matrix()
    tri = jnp.asarray(np.tril(np.ones((MIX_TM, MIX_TM), np.float32), -1), dtype=BF16)
    wr = jnp.zeros((D_MODEL, LANES), F32).at[:, :N_EXPERTS].set(w_router.astype(F32))
    wrh, wrl = _split_bf16(wr)
    br = jnp.zeros((1, LANES), F32).at[0, :N_EXPERTS].set(b_router.astype(F32))

    x2d = x.reshape(t, d)
    for l in range(depth):
        proj = _inproj(x2d, w_in[l].astype(BF16))
        ya = _attn_a(proj, _attn_bias_tiles(rel_bias[l]), batch, seq)
        yb = _attn_b(proj, scan_m, batch, seq)
        x1ext, counts = _mix(ya, yb, proj, x2d, w_up_a[l].astype(BF16), w_up_b[l].astype(BF16),
                             w_out[l].astype(BF16), ln1_g[l].reshape(1, d), ln1_b[l].reshape(1, d),
                             wrh, wrl, br, tri, alpha)
        tile_lo, tile_hi, nvalid, ntiles, src = _routing_tables(x1ext, counts, n_tiles_max)
        x2d = _moe(x1ext, tile_lo, tile_hi, nvalid, ntiles, src,
                   w_gate[l].astype(BF16), w_up[l].astype(BF16), w_down[l].astype(BF16),
                   ln2_g[l].reshape(1, d), ln2_b[l].reshape(1, d), alpha)
    return x2d.reshape(batch, seq, d)
```

```python
import functools

import numpy as np
import jax
import jax.numpy as jnp
from jax import lax
from jax.experimental import pallas as pl
from jax.experimental.pallas import tpu as pltpu

F32 = jnp.float32
BF16 = jnp.bfloat16

D_MODEL = 1024
HEAD_DIM = 64
N_HEADS = 8
WIDTH = N_HEADS * HEAD_DIM
CHUNK = 64
LEFT_CHUNKS = 8
MAX_REL = 128
N_EXPERTS = 16
N_GROUPS = 4
EXPERTS_PER_GROUP = 4
D_EXPERT = 512
LN_EPS = 1e-5
PROJ_COLS = 6 * WIDTH + 2 * D_MODEL
N_PAIRS = EXPERTS_PER_GROUP * (EXPERTS_PER_GROUP - 1) // 2
N_CLASSES = N_GROUPS * N_PAIRS

LANES = 128
HEADS_PER_LANE_BLOCK = LANES // HEAD_DIM
N_HEAD_PAIRS = N_HEADS // HEADS_PER_LANE_BLOCK
VMEM_LIMIT = 56 * 1024 * 1024

PROJ_TM = 512
PROJ_TN = 512
ATT_TQ = 256
ATT_WIN = ATT_TQ + LEFT_CHUNKS * CHUNK
SB_BLK = 128
SB_EXIT = -110.0
MIX_TM = 256
MOE_TE = 256
MOE_ROW_UNROLL = 8
META_W = LANES

NEG_BIG = -1e30


def _cparams(sem):
    return pltpu.CompilerParams(dimension_semantics=sem, vmem_limit_bytes=VMEM_LIMIT)


def _inproj_kernel(x_ref, w_ref, o_ref, *, scale):
    xb = x_ref[...].astype(BF16)
    n_chunks = PROJ_COLS // PROJ_TN
    q_chunks = (0, 3 * WIDTH // PROJ_TN)
    gate_start = 6 * WIDTH // PROJ_TN
    for c in range(n_chunks):
        cols = slice(c * PROJ_TN, (c + 1) * PROJ_TN)
        acc = jnp.dot(xb, w_ref[:, cols], preferred_element_type=F32)
        if c in q_chunks:
            acc = acc * scale
        elif c >= gate_start:
            acc = jax.nn.sigmoid(acc)
        o_ref[:, cols] = acc.astype(BF16)


def _inproj(x2d, w_bf16):
    t = x2d.shape[0]
    assert PROJ_TN == WIDTH
    return pl.pallas_call(
        functools.partial(_inproj_kernel, scale=HEAD_DIM ** -0.5),
        grid=(t // PROJ_TM,),
        in_specs=[pl.BlockSpec((PROJ_TM, D_MODEL), lambda i: (i, 0)),
                  pl.BlockSpec((D_MODEL, PROJ_COLS), lambda i: (0, 0))],
        out_specs=pl.BlockSpec((PROJ_TM, PROJ_COLS), lambda i: (i, 0)),
        out_shape=jax.ShapeDtypeStruct((t, PROJ_COLS), BF16),
        compiler_params=_cparams(("parallel",)),
    )(x2d, w_bf16)


def _attn_bias_tiles(rel_bias):
    depth = rel_bias.shape[0]
    period = ATT_TQ + ATT_WIN
    m = np.arange(period)
    key_minus_query = np.where(m < ATT_WIN, m, m - period)
    cases = np.arange(3)[:, None] * ATT_TQ
    idx = np.clip(cases - key_minus_query[None, :], -MAX_REL, MAX_REL) + MAX_REL
    diag = jnp.take(rel_bias.astype(F32), jnp.asarray(idx.reshape(-1)), axis=-1)
    diag = diag.reshape(depth, N_HEADS, 3, period).transpose(0, 2, 1, 3)
    flat = jnp.tile(diag, (1, 1, 1, ATT_TQ))[..., :ATT_TQ * (period - 1)]
    tiles = flat.reshape(depth, 3, N_HEADS, ATT_TQ, period - 1)[..., :ATT_WIN]

    qpos = cases[:, :, None] + np.arange(ATT_TQ)[None, :, None]
    kpos = np.arange(ATT_WIN)[None, None, :]
    qc, kc = qpos // CHUNK, kpos // CHUNK
    allowed = (kc <= qc) & (kc >= qc - LEFT_CHUNKS)
    return jnp.where(jnp.asarray(allowed)[None, :, None], tiles, NEG_BIG)


def _attn_a_kernel(q_ref, k_ref, v_ref, bias_ref, o_ref):
    qt = pl.program_id(2)
    start = pl.multiple_of(jnp.maximum(qt * ATT_TQ - LEFT_CHUNKS * CHUNK, 0), ATT_TQ)
    q = q_ref[...]
    kw = k_ref[pl.ds(start, ATT_WIN), :]
    vw = v_ref[pl.ds(start, ATT_WIN), :]
    first_head = lax.broadcasted_iota(jnp.int32, (1, LANES), 1) < HEAD_DIM
    outs = []
    for h in range(HEADS_PER_LANE_BLOCK):
        head_lanes = first_head if h == 0 else jnp.logical_not(first_head)
        qh = jnp.where(head_lanes, q, jnp.zeros_like(q))
        s = lax.dot_general(qh, kw, (((1,), (1,)), ((), ())), preferred_element_type=F32)
        s = s + bias_ref[0, h]
        m = jnp.max(s, axis=-1, keepdims=True)
        p = jnp.exp(s - m)
        l = jnp.sum(p, axis=-1, keepdims=True)
        o = jnp.dot(p.astype(BF16), vw, preferred_element_type=F32)
        outs.append(o / l)
    o_ref[...] = jnp.where(first_head, outs[0], outs[1]).astype(BF16)


def _attn_a(proj, bias_tiles, batch, seq):
    n_qt = seq // ATT_TQ
    q_col, k_col, v_col = 0, WIDTH // LANES, 2 * WIDTH // LANES
    last_case = bias_tiles.shape[0] - 1
    return pl.pallas_call(
        _attn_a_kernel,
        grid=(batch, N_HEAD_PAIRS, n_qt),
        in_specs=[
            pl.BlockSpec((ATT_TQ, LANES), lambda b, hp, qt: (b * n_qt + qt, q_col + hp)),
            pl.BlockSpec((seq, LANES), lambda b, hp, qt: (b, k_col + hp)),
            pl.BlockSpec((seq, LANES), lambda b, hp, qt: (b, v_col + hp)),
            pl.BlockSpec((1, HEADS_PER_LANE_BLOCK, ATT_TQ, ATT_WIN),
                         lambda b, hp, qt: (jnp.minimum(qt, last_case), hp, 0, 0)),
        ],
        out_specs=pl.BlockSpec((ATT_TQ, LANES), lambda b, hp, qt: (b * n_qt + qt, hp)),
        out_shape=jax.ShapeDtypeStruct((batch * seq, WIDTH), BF16),
        compiler_params=_cparams(("parallel", "parallel", "arbitrary")),
    )(proj, proj, proj, bias_tiles)


def _sb_scan_matrix():
    jp = np.arange(SB_BLK)[:, None]
    c = np.arange(2 * SB_BLK)[None, :]
    half = np.where(c < SB_BLK, jp > c, True)
    return jnp.asarray(np.concatenate([half, half], axis=0), dtype=BF16)


def _sb_kernel(q_ref, k_ref, v_ref, m_ref, o_ref, acc_ref, carry_ref):
    qb = pl.program_id(1)
    first_head = lax.broadcasted_iota(jnp.int32, (1, LANES), 1) < HEAD_DIM
    row = lax.broadcasted_iota(jnp.int32, (SB_BLK, 2 * SB_BLK), 0)
    col = lax.broadcasted_iota(jnp.int32, (SB_BLK, 2 * SB_BLK), 1)
    before2 = (col & (SB_BLK - 1)) < row

    acc_ref[...] = jnp.zeros_like(acc_ref)
    carry_ref[...] = jnp.zeros_like(carry_ref)

    def block(kb, diag):
        koff = pl.multiple_of(kb * SB_BLK, SB_BLK)
        worst = None
        for hp in range(N_HEAD_PAIRS):
            lanes = slice(hp * LANES, (hp + 1) * LANES)
            q = q_ref[:, lanes]
            k = k_ref[pl.ds(koff, SB_BLK), lanes]
            v = v_ref[pl.ds(koff, SB_BLK), lanes]
            zero = jnp.zeros_like(k)
            k2 = jnp.concatenate([jnp.where(first_head, k, zero), jnp.where(first_head, zero, k)], axis=0)
            v2 = jnp.concatenate([jnp.where(first_head, v, zero), jnp.where(first_head, zero, v)], axis=0)
            z = lax.dot_general(q, k2, (((1,), (1,)), ((), ())), preferred_element_type=F32)
            t = jnp.log(1.0 + jnp.exp(-jnp.abs(z)))
            log_beta = jnp.minimum(z, 0.0) - t
            log_keep = jnp.minimum(-z, 0.0) - t
            if diag:
                log_keep = jnp.where(before2, log_keep, 0.0)
            a_heads = []
            for h in range(HEADS_PER_LANE_BLOCK):
                hl = slice(h * SB_BLK, (h + 1) * SB_BLK)
                lk = log_keep[:, hl]
                hi = lk.astype(BF16)
                lo = (lk - hi.astype(F32)).astype(BF16)
                st = jnp.dot(jnp.concatenate([hi, lo], axis=1), m_ref[...], preferred_element_type=F32)
                idx = hp * HEADS_PER_LANE_BLOCK + h
                carry = carry_ref[idx]
                a_heads.append(jnp.exp(log_beta[:, hl] + st[:, :SB_BLK] + carry))
                carry = carry + st[:, SB_BLK:]
                carry_ref[idx] = carry
                worst = carry if worst is None else jnp.maximum(worst, carry)
            a = jnp.concatenate(a_heads, axis=1)
            if diag:
                a = jnp.where(before2, a, 0.0)
            acc_ref[hp] += jnp.dot(a.astype(BF16), v2, preferred_element_type=F32)
        return jnp.max(worst)

    worst0 = block(qb, True)

    def cond(state):
        kb, worst = state
        return jnp.logical_and(kb >= 0, worst >= SB_EXIT)

    def body(state):
        kb, _ = state
        return kb - 1, block(kb, False)

    lax.while_loop(cond, body, (qb - 1, worst0))

    for hp in range(N_HEAD_PAIRS):
        o_ref[:, hp * LANES:(hp + 1) * LANES] = acc_ref[hp].astype(BF16)


def _attn_b(proj, scan_m, batch, seq):
    n_qb = seq // SB_BLK
    q_col, k_col, v_col = 3, 4, 5
    return pl.pallas_call(
        _sb_kernel,
        grid=(batch, n_qb),
        in_specs=[
            pl.BlockSpec((SB_BLK, WIDTH), lambda b, qb: (b * n_qb + qb, q_col)),
            pl.BlockSpec((seq, WIDTH), lambda b, qb: (b, k_col)),
            pl.BlockSpec((seq, WIDTH), lambda b, qb: (b, v_col)),
            pl.BlockSpec((2 * SB_BLK, 2 * SB_BLK), lambda b, qb: (0, 0)),
        ],
        out_specs=pl.BlockSpec((SB_BLK, WIDTH), lambda b, qb: (b * n_qb + qb, 0)),
        out_shape=jax.ShapeDtypeStruct((batch * seq, WIDTH), BF16),
        scratch_shapes=[pltpu.VMEM((N_HEAD_PAIRS, SB_BLK, LANES), F32),
                        pltpu.VMEM((N_HEADS, SB_BLK, SB_BLK), F32)],
        compiler_params=_cparams(("parallel", "arbitrary")),
    )(proj, proj, proj, scan_m)


def _layer_norm(h, g, b):
    mu = jnp.mean(h, axis=-1, keepdims=True)
    d = h - mu
    var = jnp.mean(d * d, axis=-1, keepdims=True)
    return d * lax.rsqrt(var + LN_EPS) * g + b


def _split_bf16(x):
    hi = x.astype(BF16)
    return hi, (x - hi.astype(F32)).astype(BF16)


def _mix_kernel(ya_ref, yb_ref, ga_ref, gb_ref, x_ref, wa_ref, wb_ref, wo_ref, g_ref, b_ref,
                wrh_ref, wrl_ref, br_ref, tri_ref, o_ref, cnt_ref, run_ref, *, alpha):
    @pl.when(pl.program_id(0) == 0)
    def _():
        run_ref[...] = jnp.zeros_like(run_ref)

    pa = jnp.dot(ya_ref[...], wa_ref[...], preferred_element_type=F32)
    pb = jnp.dot(yb_ref[...], wb_ref[...], preferred_element_type=F32)
    merged = ga_ref[...].astype(F32) * pa + gb_ref[...].astype(F32) * pb
    mixed = jnp.dot(merged.astype(BF16), wo_ref[...], preferred_element_type=F32)
    x1 = _layer_norm(alpha * x_ref[...] + mixed, g_ref[...], b_ref[...])
    o_ref[:, :D_MODEL] = x1

    xh, xl = _split_bf16(x1)
    logits = (jnp.dot(xh, wrh_ref[...], preferred_element_type=F32)
              + jnp.dot(xl, wrh_ref[...], preferred_element_type=F32)
              + jnp.dot(xh, wrl_ref[...], preferred_element_type=F32)
              + jnp.dot(xl, wrl_ref[...], preferred_element_type=F32)) + br_ref[...]
    lane = lax.broadcasted_iota(jnp.int32, logits.shape, 1)
    logits = jnp.where(lane < N_EXPERTS, logits, NEG_BIG)
    e = jnp.exp(logits - jnp.max(logits, axis=-1, keepdims=True))
    probs = e / jnp.sum(e, axis=-1, keepdims=True)
    p1 = jnp.max(probs, axis=-1, keepdims=True)
    i1 = jnp.min(jnp.where(probs == p1, lane, LANES), axis=-1, keepdims=True)
    group = i1 >> 2
    rest = jnp.where(jnp.logical_and(lane >> 2 == group, lane != i1), probs, -1.0)
    p2 = jnp.max(rest, axis=-1, keepdims=True)
    i2 = jnp.min(jnp.where(rest == p2, lane, LANES), axis=-1, keepdims=True)
    w1 = p1 / (p1 + p2)
    w2 = p2 / (p1 + p2)
    first_low = i1 < i2
    e_lo = jnp.where(first_low, i1, i2) & 3
    e_hi = jnp.where(first_low, i2, i1) & 3
    w_lo = jnp.where(first_low, w1, w2)
    w_hi = jnp.where(first_low, w2, w1)
    cls = group * N_PAIRS + ((e_lo * (5 - e_lo)) >> 1) + e_hi - 1

    onehot = (lane == cls).astype(F32)
    within = jnp.dot(tri_ref[...], onehot.astype(BF16), preferred_element_type=F32)
    rank = jnp.sum((within + run_ref[...]) * onehot, axis=-1, keepdims=True)
    run_ref[...] = run_ref[...] + jnp.sum(onehot, axis=0, keepdims=True)
    cnt_ref[...] = jnp.broadcast_to(run_ref[...], cnt_ref.shape)

    meta = jnp.where(lane == 0, cls.astype(F32),
                     jnp.where(lane == 1, rank,
                               jnp.where(lane == 2, w_lo, jnp.where(lane == 3, w_hi, 0.0))))
    o_ref[:, D_MODEL:] = meta


def _mix(ya, yb, proj, x2d, wa, wb, wo, g, b, wrh, wrl, br, tri, alpha):
    t = x2d.shape[0]
    ga_col = 6 * WIDTH // D_MODEL
    const = lambda shape: pl.BlockSpec(shape, lambda i: (0,) * len(shape))
    return pl.pallas_call(
        functools.partial(_mix_kernel, alpha=alpha),
        grid=(t // MIX_TM,),
        in_specs=[
            pl.BlockSpec((MIX_TM, WIDTH), lambda i: (i, 0)),
            pl.BlockSpec((MIX_TM, WIDTH), lambda i: (i, 0)),
            pl.BlockSpec((MIX_TM, D_MODEL), lambda i: (i, ga_col)),
            pl.BlockSpec((MIX_TM, D_MODEL), lambda i: (i, ga_col + 1)),
            pl.BlockSpec((MIX_TM, D_MODEL), lambda i: (i, 0)),
            const((WIDTH, D_MODEL)), const((WIDTH, D_MODEL)), const((D_MODEL, D_MODEL)),
            const((1, D_MODEL)), const((1, D_MODEL)),
            const((D_MODEL, LANES)), const((D_MODEL, LANES)), const((1, LANES)),
            const((MIX_TM, MIX_TM)),
        ],
        out_specs=[pl.BlockSpec((MIX_TM, D_MODEL + META_W), lambda i: (i, 0)),
                   pl.BlockSpec((8, LANES), lambda i: (0, 0))],
        out_shape=[jax.ShapeDtypeStruct((t, D_MODEL + META_W), F32),
                   jax.ShapeDtypeStruct((8, LANES), F32)],
        scratch_shapes=[pltpu.VMEM((1, LANES), F32)],
        compiler_params=_cparams(("arbitrary",)),
    )(ya, yb, proj, proj, x2d, wa, wb, wo, g, b, wrh, wrl, br, tri)


def _moe_kernel(tlo_ref, thi_ref, nvalid_ref, ntiles_ref,
                src_ref, nxt_ref, x_hbm, wg0_ref, wu0_ref, wd0_ref, wg1_ref, wu1_ref, wd1_ref, g_ref, b_ref,
                out_hbm, xg_ref, y_ref, sem_in, sem_out, *, alpha):
    i = pl.program_id(0)
    ntiles = ntiles_ref[0]
    slot = lax.rem(i, 2)

    def unrolled_rows(fn):
        def body(c, carry):
            base = pl.multiple_of(c * MOE_ROW_UNROLL, MOE_ROW_UNROLL)
            for u in range(MOE_ROW_UNROLL):
                fn(base + u)
            return carry
        lax.fori_loop(0, MOE_TE // MOE_ROW_UNROLL, body, 0)

    def dynamic_rows(n, fn):
        def body(r, carry):
            fn(r)
            return carry
        lax.fori_loop(0, n, body, 0)

    def gather_copy(idx_ref, s, r):
        return pltpu.make_async_copy(x_hbm.at[pl.ds(idx_ref[0, 0, r], 1)], xg_ref.at[s, pl.ds(r, 1)], sem_in.at[s])

    def scatter_copy(s, r):
        return pltpu.make_async_copy(y_ref.at[s, pl.ds(r, 1)], out_hbm.at[pl.ds(src_ref[0, 0, r], 1)], sem_out.at[s])

    def scatter_wait_copy(s, r):
        return pltpu.make_async_copy(y_ref.at[s, pl.ds(r, 1)], out_hbm.at[pl.ds(0, 1)], sem_out.at[s])

    def for_valid_rows(n, fn):
        @pl.when(n == MOE_TE)
        def _():
            unrolled_rows(fn)

        @pl.when(n != MOE_TE)
        def _():
            dynamic_rows(n, fn)

    @pl.when(i == 0)
    def _():
        unrolled_rows(lambda r: gather_copy(src_ref, 0, r).start())

    @pl.when(i + 1 < ntiles)
    def _():
        unrolled_rows(lambda r: gather_copy(nxt_ref, 1 - slot, r).start())

    @pl.when(i < ntiles)
    def _():
        unrolled_rows(lambda r: gather_copy(src_ref, slot, r).wait())

        @pl.when(i >= 2)
        def _():
            for_valid_rows(nvalid_ref[jnp.maximum(i - 2, 0)], lambda r: scatter_wait_copy(slot, r).wait())

        xg = xg_ref[slot]
        x = xg[:, :D_MODEL]
        xb = x.astype(BF16)
        w_lo = xg[:, D_MODEL + 2:D_MODEL + 3]
        w_hi = xg[:, D_MODEL + 3:D_MODEL + 4]

        def expert(wg_ref, wu_ref, wd_ref):
            gate = jnp.dot(xb, wg_ref[0], preferred_element_type=F32)
            up = jnp.dot(xb, wu_ref[0], preferred_element_type=F32)
            hidden = (gate * jax.nn.sigmoid(gate) * up).astype(BF16)
            return jnp.dot(hidden, wd_ref[0], preferred_element_type=F32)

        ffn = w_lo * expert(wg0_ref, wu0_ref, wd0_ref) + w_hi * expert(wg1_ref, wu1_ref, wd1_ref)
        y_ref[slot] = _layer_norm(alpha * x + ffn, g_ref[...], b_ref[...])

        for_valid_rows(nvalid_ref[i], lambda r: scatter_copy(slot, r).start())

        @pl.when(i == ntiles - 1)
        def _():
            @pl.when(i >= 1)
            def _():
                for_valid_rows(nvalid_ref[jnp.maximum(i - 1, 0)], lambda r: scatter_wait_copy(1 - slot, r).wait())

            for_valid_rows(nvalid_ref[i], lambda r: scatter_wait_copy(slot, r).wait())


def _moe(x1ext, tile_lo, tile_hi, tile_nvalid, ntiles, src, wg, wu, wd, g, b, alpha):
    t = x1ext.shape[0]
    n_tiles_max = src.shape[0]
    wspec_in = lambda which: pl.BlockSpec(
        (1, D_MODEL, D_EXPERT), lambda i, tlo, thi, nv, nt: ((tlo, thi)[which][i], 0, 0))
    wspec_out = lambda which: pl.BlockSpec(
        (1, D_EXPERT, D_MODEL), lambda i, tlo, thi, nv, nt: ((tlo, thi)[which][i], 0, 0))
    vec = pl.BlockSpec((1, D_MODEL), lambda i, *_: (0, 0))
    grid_spec = pltpu.PrefetchScalarGridSpec(
        num_scalar_prefetch=4,
        grid=(n_tiles_max,),
        in_specs=[
            pl.BlockSpec((1, 1, MOE_TE), lambda i, *_: (i, 0, 0), memory_space=pltpu.SMEM),
            pl.BlockSpec((1, 1, MOE_TE), lambda i, *_: (jnp.minimum(i + 1, n_tiles_max - 1), 0, 0),
                         memory_space=pltpu.SMEM),
            pl.BlockSpec(memory_space=pl.ANY),
            wspec_in(0), wspec_in(0), wspec_out(0), wspec_in(1), wspec_in(1), wspec_out(1),
            vec, vec,
        ],
        out_specs=pl.BlockSpec(memory_space=pl.ANY),
        scratch_shapes=[pltpu.VMEM((2, MOE_TE, D_MODEL + META_W), F32),
                        pltpu.VMEM((2, MOE_TE, D_MODEL), F32),
                        pltpu.SemaphoreType.DMA((2,)), pltpu.SemaphoreType.DMA((2,))],
    )
    return pl.pallas_call(
        functools.partial(_moe_kernel, alpha=alpha),
        grid_spec=grid_spec,
        out_shape=jax.ShapeDtypeStruct((t, D_MODEL), F32),
        compiler_params=_cparams(("arbitrary",)),
    )(tile_lo, tile_hi, tile_nvalid, ntiles, src, src, x1ext, wg, wu, wd, wg, wu, wd, g, b)


def _class_experts():
    lo, hi = [], []
    for grp in range(N_GROUPS):
        for a in range(EXPERTS_PER_GROUP):
            for c in range(a + 1, EXPERTS_PER_GROUP):
                lo.append(grp * EXPERTS_PER_GROUP + a)
                hi.append(grp * EXPERTS_PER_GROUP + c)
    return np.asarray(lo, np.int32), np.asarray(hi, np.int32)


def _routing_tables(x1ext, counts, n_tiles_max):
    t = x1ext.shape[0]
    cls = x1ext[:, D_MODEL].astype(jnp.int32)
    rank = x1ext[:, D_MODEL + 1].astype(jnp.int32)
    cnt = counts[0, :N_CLASSES].astype(jnp.int32)
    padded = ((cnt + MOE_TE - 1) // MOE_TE) * MOE_TE
    ends = jnp.cumsum(padded)
    offs = ends - padded
    dest = offs[cls] + rank
    src = jnp.zeros((n_tiles_max * MOE_TE,), jnp.int32).at[dest].set(jnp.arange(t, dtype=jnp.int32))
    tile_start = jnp.arange(n_tiles_max, dtype=jnp.int32) * MOE_TE
    tile_cls = jnp.sum((ends[None, :] <= tile_start[:, None]).astype(jnp.int32), axis=1)
    tile_cls = jnp.minimum(tile_cls, N_CLASSES - 1)
    nvalid = jnp.clip(offs[tile_cls] + cnt[tile_cls] - tile_start, 0, MOE_TE).astype(jnp.int32)
    ntiles = (ends[-1] // MOE_TE).astype(jnp.int32).reshape(1)
    e_lo, e_hi = _class_experts()
    return (jnp.asarray(e_lo)[tile_cls], jnp.asarray(e_hi)[tile_cls], nvalid, ntiles,
            src.reshape(n_tiles_max, 1, MOE_TE))


def kernel(x, w_in, rel_bias, w_up_a, w_up_b, w_out, ln1_g, ln1_b, w_router, b_router,
           w_gate, w_up, w_down, ln2_g, ln2_b):
    batch, seq, d = x.shape
    depth = w_in.shape[0]
    assert d == D_MODEL and seq % ATT_TQ == 0 and seq >= ATT_WIN
    t = batch * seq
    assert t % PROJ_TM == 0 and t % MIX_TM == 0
    alpha = (2 * depth) ** 0.25
    n_tiles_max = t // MOE_TE + N_CLASSES

    scan_m = _sb_scan_matrix()
    tri = jnp.asarray(np.tril(np.ones((MIX_TM, MIX_TM), np.float32), -1), dtype=BF16)
    wr = jnp.zeros((D_MODEL, LANES), F32).at[:, :N_EXPERTS].set(w_router.astype(F32))
    wrh, wrl = _split_bf16(wr)
    br = jnp.zeros((1, LANES), F32).at[0, :N_EXPERTS].set(b_router.astype(F32))

    bias_tiles = _attn_bias_tiles(rel_bias)

    x2d = x.reshape(t, d)
    for l in range(depth):
        proj = _inproj(x2d, w_in[l].astype(BF16))
        ya = _attn_a(proj, bias_tiles[l], batch, seq)
        yb = _attn_b(proj, scan_m, batch, seq)
        x1ext, counts = _mix(ya, yb, proj, x2d, w_up_a[l].astype(BF16), w_up_b[l].astype(BF16),
                             w_out[l].astype(BF16), ln1_g[l].reshape(1, d), ln1_b[l].reshape(1, d),
                             wrh, wrl, br, tri, alpha)
        tile_lo, tile_hi, nvalid, ntiles, src = _routing_tables(x1ext, counts, n_tiles_max)
        x2d = _moe(x1ext, tile_lo, tile_hi, nvalid, ntiles, src,
                   w_gate[l].astype(BF16), w_up[l].astype(BF16), w_down[l].astype(BF16),
                   ln2_g[l].reshape(1, d), ln2_b[l].reshape(1, d), alpha)
    return x2d.reshape(batch, seq, d)
```

```python
import functools

import numpy as np
import jax
import jax.numpy as jnp
from jax import lax
from jax.experimental import pallas as pl
from jax.experimental.pallas import tpu as pltpu

F32 = jnp.float32
BF16 = jnp.bfloat16

D_MODEL = 1024
HEAD_DIM = 64
N_HEADS = 8
WIDTH = N_HEADS * HEAD_DIM
CHUNK = 64
LEFT_CHUNKS = 8
MAX_REL = 128
N_EXPERTS = 16
N_GROUPS = 4
EXPERTS_PER_GROUP = 4
D_EXPERT = 512
LN_EPS = 1e-5
PROJ_COLS = 6 * WIDTH + 2 * D_MODEL
N_PAIRS = EXPERTS_PER_GROUP * (EXPERTS_PER_GROUP - 1) // 2
N_CLASSES = N_GROUPS * N_PAIRS

LANES = 128
HEADS_PER_LANE_BLOCK = LANES // HEAD_DIM
N_HEAD_PAIRS = N_HEADS // HEADS_PER_LANE_BLOCK
VMEM_LIMIT = 56 * 1024 * 1024

PROJ_TM = 512
PROJ_TN = 512
ATT_TQ = 256
ATT_WIN = ATT_TQ + LEFT_CHUNKS * CHUNK
SB_Q = 256
SB_BLK = 128
SB_EXIT = -110.0
MIX_TM = 512
MOE_TE = 256
MOE_ROW_UNROLL = 8
META_W = LANES

NEG_BIG = -1e30


def _cparams(sem):
    return pltpu.CompilerParams(dimension_semantics=sem, vmem_limit_bytes=VMEM_LIMIT)


def _inproj_kernel(x_ref, w_ref, o_ref, *, scale):
    xb = x_ref[...].astype(BF16)
    n_chunks = PROJ_COLS // PROJ_TN
    q_chunks = (0, 3 * WIDTH // PROJ_TN)
    gate_start = 6 * WIDTH // PROJ_TN
    for c in range(n_chunks):
        cols = slice(c * PROJ_TN, (c + 1) * PROJ_TN)
        acc = jnp.dot(xb, w_ref[:, cols], preferred_element_type=F32)
        if c in q_chunks:
            acc = acc * scale
        elif c >= gate_start:
            acc = jax.nn.sigmoid(acc)
        o_ref[:, cols] = acc.astype(BF16)


def _inproj(x2d, w_bf16, t):
    assert PROJ_TN == WIDTH
    return pl.pallas_call(
        functools.partial(_inproj_kernel, scale=HEAD_DIM ** -0.5),
        grid=(t // PROJ_TM,),
        in_specs=[pl.BlockSpec((PROJ_TM, D_MODEL), lambda i: (i, 0)),
                  pl.BlockSpec((D_MODEL, PROJ_COLS), lambda i: (0, 0))],
        out_specs=pl.BlockSpec((PROJ_TM, PROJ_COLS), lambda i: (i, 0)),
        out_shape=jax.ShapeDtypeStruct((t, PROJ_COLS), BF16),
        compiler_params=_cparams(("parallel",)),
    )(x2d, w_bf16)


def _attn_bias_tiles(rel_bias):
    depth = rel_bias.shape[0]
    period = ATT_TQ + ATT_WIN
    m = np.arange(period)
    key_minus_query = np.where(m < ATT_WIN, m, m - period)
    cases = np.arange(3)[:, None] * ATT_TQ
    idx = np.clip(cases - key_minus_query[None, :], -MAX_REL, MAX_REL) + MAX_REL
    diag = jnp.take(rel_bias.astype(F32), jnp.asarray(idx.reshape(-1)), axis=-1)
    diag = diag.reshape(depth, N_HEADS, 3, period).transpose(0, 2, 1, 3)
    flat = jnp.tile(diag, (1, 1, 1, ATT_TQ))[..., :ATT_TQ * (period - 1)]
    tiles = flat.reshape(depth, 3, N_HEADS, ATT_TQ, period - 1)[..., :ATT_WIN]

    qpos = cases[:, :, None] + np.arange(ATT_TQ)[None, :, None]
    kpos = np.arange(ATT_WIN)[None, None, :]
    qc, kc = qpos // CHUNK, kpos // CHUNK
    allowed = (kc <= qc) & (kc >= qc - LEFT_CHUNKS)
    return jnp.where(jnp.asarray(allowed)[None, :, None], tiles, NEG_BIG)


def _attn_a_kernel(q_ref, k_ref, v_ref, bias_ref, o_ref):
    qt = pl.program_id(1)
    start = pl.multiple_of(jnp.maximum(qt * ATT_TQ - LEFT_CHUNKS * CHUNK, 0), ATT_TQ)
    first_head = lax.broadcasted_iota(jnp.int32, (1, LANES), 1) < HEAD_DIM
    for hp in range(N_HEAD_PAIRS):
        lanes = slice(hp * LANES, (hp + 1) * LANES)
        q = q_ref[:, lanes]
        kw = k_ref[pl.ds(start, ATT_WIN), lanes]
        vw = v_ref[pl.ds(start, ATT_WIN), lanes]
        outs = []
        for h in range(HEADS_PER_LANE_BLOCK):
            head_lanes = first_head if h == 0 else jnp.logical_not(first_head)
            qh = jnp.where(head_lanes, q, jnp.zeros_like(q))
            s = lax.dot_general(qh, kw, (((1,), (1,)), ((), ())), preferred_element_type=F32)
            s = s + bias_ref[0, hp * HEADS_PER_LANE_BLOCK + h]
            m = jnp.max(s, axis=-1, keepdims=True)
            p = jnp.exp(s - m)
            l = jnp.sum(p, axis=-1, keepdims=True)
            o = jnp.dot(p.astype(BF16), vw, preferred_element_type=F32)
            outs.append(o / l)
        o_ref[:, lanes] = jnp.where(first_head, outs[0], outs[1]).astype(BF16)


def _attn_a(proj, bias_tiles, batch, seq):
    n_qt = seq // ATT_TQ
    q_col, k_col, v_col = 0, 1, 2
    last_case = bias_tiles.shape[0] - 1
    return pl.pallas_call(
        _attn_a_kernel,
        grid=(batch, n_qt),
        in_specs=[
            pl.BlockSpec((ATT_TQ, WIDTH), lambda b, qt: (b * n_qt + qt, q_col)),
            pl.BlockSpec((seq, WIDTH), lambda b, qt: (b, k_col)),
            pl.BlockSpec((seq, WIDTH), lambda b, qt: (b, v_col)),
            pl.BlockSpec((1, N_HEADS, ATT_TQ, ATT_WIN), lambda b, qt: (jnp.minimum(qt, last_case), 0, 0, 0)),
        ],
        out_specs=pl.BlockSpec((ATT_TQ, WIDTH), lambda b, qt: (b * n_qt + qt, 0)),
        out_shape=jax.ShapeDtypeStruct((batch * seq, WIDTH), BF16),
        compiler_params=_cparams(("parallel", "arbitrary")),
    )(proj, proj, proj, bias_tiles)


def _sb_scan_matrix():
    jp = np.arange(SB_BLK)[:, None]
    c = np.arange(2 * SB_BLK)[None, :]
    half = np.where(c < SB_BLK, jp > c, True)
    return jnp.asarray(np.concatenate([half, half], axis=0), dtype=BF16)


def _sb_kernel(q_ref, k_ref, v_ref, m_ref, o_ref, acc_ref, carry_ref):
    qb = pl.program_id(1)
    first_head = lax.broadcasted_iota(jnp.int32, (1, LANES), 1) < HEAD_DIM
    row = lax.broadcasted_iota(jnp.int32, (SB_Q, 2 * SB_BLK), 0)
    col = lax.broadcasted_iota(jnp.int32, (SB_Q, 2 * SB_BLK), 1) & (SB_BLK - 1)

    acc_ref[...] = jnp.zeros_like(acc_ref)
    carry_ref[...] = jnp.zeros_like(carry_ref)

    def block(kb, key_offset):
        diag = key_offset is not None
        if diag:
            before2 = (col + key_offset) < row
        koff = pl.multiple_of(kb * SB_BLK, SB_BLK)
        worst = None
        for hp in range(N_HEAD_PAIRS):
            lanes = slice(hp * LANES, (hp + 1) * LANES)
            q = q_ref[:, lanes]
            k = k_ref[pl.ds(koff, SB_BLK), lanes]
            v = v_ref[pl.ds(koff, SB_BLK), lanes]
            zero = jnp.zeros_like(k)
            k2 = jnp.concatenate([jnp.where(first_head, k, zero), jnp.where(first_head, zero, k)], axis=0)
            v2 = jnp.concatenate([jnp.where(first_head, v, zero), jnp.where(first_head, zero, v)], axis=0)
            z = lax.dot_general(q, k2, (((1,), (1,)), ((), ())), preferred_element_type=F32)
            t = jnp.log(1.0 + jnp.exp(-jnp.abs(z)))
            log_beta = jnp.minimum(z, 0.0) - t
            log_keep = jnp.minimum(-z, 0.0) - t
            if diag:
                log_keep = jnp.where(before2, log_keep, 0.0)
            a_heads = []
            for h in range(HEADS_PER_LANE_BLOCK):
                hl = slice(h * SB_BLK, (h + 1) * SB_BLK)
                lk = log_keep[:, hl]
                hi = lk.astype(BF16)
                lo = (lk - hi.astype(F32)).astype(BF16)
                st = jnp.dot(jnp.concatenate([hi, lo], axis=1), m_ref[...], preferred_element_type=F32)
                idx = hp * HEADS_PER_LANE_BLOCK + h
                carry = carry_ref[idx]
                a_heads.append(jnp.exp(log_beta[:, hl] + st[:, :SB_BLK] + carry))
                carry = carry + st[:, SB_BLK:]
                carry_ref[idx] = carry
                worst = carry if worst is None else jnp.maximum(worst, carry)
            a = jnp.concatenate(a_heads, axis=1)
            if diag:
                a = jnp.where(before2, a, 0.0)
            acc_ref[hp] += jnp.dot(a.astype(BF16), v2, preferred_element_type=F32)
        return jnp.max(worst)

    blocks_per_tile = SB_Q // SB_BLK
    for d in reversed(range(blocks_per_tile)):
        worst0 = block(qb * blocks_per_tile + d, d * SB_BLK)

    def cond(state):
        kb, worst = state
        return jnp.logical_and(kb >= 0, worst >= SB_EXIT)

    def body(state):
        kb, _ = state
        return kb - 1, block(kb, None)

    lax.while_loop(cond, body, (qb * blocks_per_tile - 1, worst0))

    for hp in range(N_HEAD_PAIRS):
        o_ref[:, hp * LANES:(hp + 1) * LANES] = acc_ref[hp].astype(BF16)


def _attn_b(proj, scan_m, batch, seq):
    n_qb = seq // SB_Q
    q_col, k_col, v_col = 3, 4, 5
    return pl.pallas_call(
        _sb_kernel,
        grid=(batch, n_qb),
        in_specs=[
            pl.BlockSpec((SB_Q, WIDTH), lambda b, qb: (b * n_qb + qb, q_col)),
            pl.BlockSpec((seq, WIDTH), lambda b, qb: (b, k_col)),
            pl.BlockSpec((seq, WIDTH), lambda b, qb: (b, v_col)),
            pl.BlockSpec((2 * SB_BLK, 2 * SB_BLK), lambda b, qb: (0, 0)),
        ],
        out_specs=pl.BlockSpec((SB_Q, WIDTH), lambda b, qb: (b * n_qb + qb, 0)),
        out_shape=jax.ShapeDtypeStruct((batch * seq, WIDTH), BF16),
        scratch_shapes=[pltpu.VMEM((N_HEAD_PAIRS, SB_Q, LANES), F32),
                        pltpu.VMEM((N_HEADS, SB_Q, SB_BLK), F32)],
        compiler_params=_cparams(("parallel", "arbitrary")),
    )(proj, proj, proj, scan_m)


def _layer_norm(h, g, b):
    mu = jnp.mean(h, axis=-1, keepdims=True)
    d = h - mu
    var = jnp.mean(d * d, axis=-1, keepdims=True)
    return d * lax.rsqrt(var + LN_EPS) * g + b


def _split_bf16(x):
    hi = x.astype(BF16)
    return hi, (x - hi.astype(F32)).astype(BF16)


def _mix_kernel(ya_ref, yb_ref, ga_ref, gb_ref, x_ref, wa_ref, wb_ref, wo_ref, g_ref, b_ref,
                wr_ref, br_ref, tri_ref, o_ref, cnt_ref, run_ref, *, alpha):
    @pl.when(pl.program_id(0) == 0)
    def _():
        run_ref[...] = jnp.zeros_like(run_ref)

    pa = jnp.dot(ya_ref[...], wa_ref[...], preferred_element_type=F32)
    pb = jnp.dot(yb_ref[...], wb_ref[...], preferred_element_type=F32)
    merged = ga_ref[...].astype(F32) * pa + gb_ref[...].astype(F32) * pb
    mixed = jnp.dot(merged.astype(BF16), wo_ref[...], preferred_element_type=F32)
    x1 = _layer_norm(alpha * x_ref[...] + mixed, g_ref[...], b_ref[...])
    o_ref[:, :D_MODEL] = x1

    xh, xl = _split_bf16(x1)
    rh = jnp.dot(xh, wr_ref[...], preferred_element_type=F32)
    rl = jnp.dot(xl, wr_ref[...], preferred_element_type=F32)
    logits = (rh[:, :LANES] + rl[:, :LANES]) + (rh[:, LANES:] + rl[:, LANES:]) + br_ref[...]
    lane = lax.broadcasted_iota(jnp.int32, logits.shape, 1)
    logits = jnp.where(lane < N_EXPERTS, logits, NEG_BIG)
    e = jnp.exp(logits - jnp.max(logits, axis=-1, keepdims=True))
    probs = e / jnp.sum(e, axis=-1, keepdims=True)
    p1 = jnp.max(probs, axis=-1, keepdims=True)
    i1 = jnp.min(jnp.where(probs == p1, lane, LANES), axis=-1, keepdims=True)
    group = i1 >> 2
    rest = jnp.where(jnp.logical_and(lane >> 2 == group, lane != i1), probs, -1.0)
    p2 = jnp.max(rest, axis=-1, keepdims=True)
    i2 = jnp.min(jnp.where(rest == p2, lane, LANES), axis=-1, keepdims=True)
    w1 = p1 / (p1 + p2)
    w2 = p2 / (p1 + p2)
    first_low = i1 < i2
    e_lo = jnp.where(first_low, i1, i2) & 3
    e_hi = jnp.where(first_low, i2, i1) & 3
    w_lo = jnp.where(first_low, w1, w2)
    w_hi = jnp.where(first_low, w2, w1)
    cls = group * N_PAIRS + ((e_lo * (5 - e_lo)) >> 1) + e_hi - 1

    onehot = (lane == cls).astype(F32)
    within = jnp.dot(tri_ref[...], onehot.astype(BF16), preferred_element_type=F32)
    rank = jnp.sum((within + run_ref[...]) * onehot, axis=-1, keepdims=True)
    run_ref[...] = run_ref[...] + jnp.sum(onehot, axis=0, keepdims=True)
    cnt_ref[...] = jnp.broadcast_to(run_ref[...], cnt_ref.shape)

    meta = jnp.where(lane == 0, cls.astype(F32),
                     jnp.where(lane == 1, rank,
                               jnp.where(lane == 2, w_lo, jnp.where(lane == 3, w_hi, 0.0))))
    o_ref[:, D_MODEL:] = meta


def _mix(ya, yb, proj, x2d, wa, wb, wo, g, b, wr, br, tri, alpha, t):
    ga_col = 6 * WIDTH // D_MODEL
    const = lambda shape: pl.BlockSpec(shape, lambda i: (0,) * len(shape))
    return pl.pallas_call(
        functools.partial(_mix_kernel, alpha=alpha),
        grid=(t // MIX_TM,),
        in_specs=[
            pl.BlockSpec((MIX_TM, WIDTH), lambda i: (i, 0)),
            pl.BlockSpec((MIX_TM, WIDTH), lambda i: (i, 0)),
            pl.BlockSpec((MIX_TM, D_MODEL), lambda i: (i, ga_col)),
            pl.BlockSpec((MIX_TM, D_MODEL), lambda i: (i, ga_col + 1)),
            pl.BlockSpec((MIX_TM, D_MODEL), lambda i: (i, 0)),
            const((WIDTH, D_MODEL)), const((WIDTH, D_MODEL)), const((D_MODEL, D_MODEL)),
            const((1, D_MODEL)), const((1, D_MODEL)),
            const((D_MODEL, 2 * LANES)), const((1, LANES)),
            const((MIX_TM, MIX_TM)),
        ],
        out_specs=[pl.BlockSpec((MIX_TM, D_MODEL + META_W), lambda i: (i, 0)),
                   pl.BlockSpec((8, LANES), lambda i: (0, 0))],
        out_shape=[jax.ShapeDtypeStruct((t, D_MODEL + META_W), F32),
                   jax.ShapeDtypeStruct((8, LANES), F32)],
        scratch_shapes=[pltpu.VMEM((1, LANES), F32)],
        compiler_params=_cparams(("arbitrary",)),
    )(ya, yb, proj, proj, x2d, wa, wb, wo, g, b, wr, br, tri)


def _moe_kernel(tlo_ref, thi_ref, ntiles_ref,
                gcur_ref, gnxt_ref, dprev_ref, dcur_ref, x_hbm,
                wg0_ref, wu0_ref, wd0_ref, wg1_ref, wu1_ref, wd1_ref, g_ref, b_ref,
                out_hbm, xg0_ref, xg1_ref, y0_ref, y1_ref, sem_in, sem_out, *, alpha, n_tokens):
    i = pl.program_id(0)
    ntiles = ntiles_ref[0]
    xg_refs = (xg0_ref, xg1_ref)
    y_refs = (y0_ref, y1_ref)

    def looped_rows(fn):
        def body(c, carry):
            base = pl.multiple_of(c * MOE_ROW_UNROLL, MOE_ROW_UNROLL)
            for u in range(MOE_ROW_UNROLL):
                fn(base + u)
            return carry
        lax.fori_loop(0, MOE_TE // MOE_ROW_UNROLL, body, 0)

    def gather_copy(idx_ref, s, r):
        return pltpu.make_async_copy(x_hbm.at[pl.ds(idx_ref[0, 0, r], 1)], xg_refs[s].at[pl.ds(r, 1)], sem_in.at[s])

    def scatter_copy(idx_ref, s, r):
        return pltpu.make_async_copy(y_refs[s].at[pl.ds(r, 1)], out_hbm.at[pl.ds(idx_ref[0, 0, r], 1)], sem_out.at[s])

    @pl.when(i == 0)
    def _():
        y1_ref[...] = jnp.zeros_like(y1_ref)
        fill = pltpu.make_async_copy(y1_ref, out_hbm.at[pl.ds(n_tokens, MOE_TE)], sem_out.at[1])
        fill.start()
        fill.wait()
        looped_rows(lambda r: gather_copy(gcur_ref, 0, r).start())

    def step(p):
        q = 1 - p
        xg_ref, y_ref = xg_refs[p], y_refs[p]
        looped_rows(lambda r: gather_copy(gcur_ref, p, r).wait())

        @pl.when(i >= 1)
        def _():
            looped_rows(lambda r: scatter_copy(dcur_ref, p, r).wait())

        n_groups = 8
        group_rows = MOE_TE // n_groups
        groups = iter(range(n_groups))

        def issue_group():
            base = next(groups) * group_rows
            for r in range(base, base + group_rows):
                gather_copy(gnxt_ref, q, r).start()
                scatter_copy(dprev_ref, q, r).start()

        xg = xg_ref[...]
        x = xg[:, :D_MODEL]
        xb = x.astype(BF16)
        w_lo = xg[:, D_MODEL + 2:D_MODEL + 3]
        w_hi = xg[:, D_MODEL + 3:D_MODEL + 4]

        def expert(wg_ref, wu_ref, wd_ref):
            issue_group()
            gate = jnp.dot(xb, wg_ref[0], preferred_element_type=F32)
            issue_group()
            up = jnp.dot(xb, wu_ref[0], preferred_element_type=F32)
            issue_group()
            hidden = (gate * jax.nn.sigmoid(gate) * up).astype(BF16)
            issue_group()
            return jnp.dot(hidden, wd_ref[0], preferred_element_type=F32)

        ffn = w_lo * expert(wg0_ref, wu0_ref, wd0_ref) + w_hi * expert(wg1_ref, wu1_ref, wd1_ref)
        y_ref[...] = _layer_norm(alpha * x + ffn, g_ref[...], b_ref[...])

        @pl.when(i == ntiles - 1)
        def _():
            looped_rows(lambda r: gather_copy(gnxt_ref, q, r).wait())
            looped_rows(lambda r: scatter_copy(dcur_ref, p, r).start())
            looped_rows(lambda r: scatter_copy(dprev_ref, q, r).wait())
            looped_rows(lambda r: scatter_copy(dcur_ref, p, r).wait())

    @pl.when(i < ntiles)
    def _():
        parity = lax.rem(i, 2)
        for p in range(2):
            pl.when(parity == p)(functools.partial(step, p))


def _moe(x1ext, tile_lo, tile_hi, ntiles, gsrc, sdst, wg, wu, wd, g, b, alpha, n_tokens):
    n_tiles_max = gsrc.shape[0] - 1
    wspec_in = lambda which: pl.BlockSpec(
        (1, D_MODEL, D_EXPERT), lambda i, tlo, thi, nt: ((tlo, thi)[which][i], 0, 0))
    wspec_out = lambda which: pl.BlockSpec(
        (1, D_EXPERT, D_MODEL), lambda i, tlo, thi, nt: ((tlo, thi)[which][i], 0, 0))
    vec = pl.BlockSpec((1, D_MODEL), lambda i, *_: (0, 0))
    rows = lambda shift: pl.BlockSpec((1, 1, MOE_TE), lambda i, *_: (i + shift, 0, 0), memory_space=pltpu.SMEM)
    grid_spec = pltpu.PrefetchScalarGridSpec(
        num_scalar_prefetch=3,
        grid=(n_tiles_max,),
        in_specs=[
            rows(0), rows(1),
            rows(0), rows(1),
            pl.BlockSpec(memory_space=pl.ANY),
            wspec_in(0), wspec_in(0), wspec_out(0), wspec_in(1), wspec_in(1), wspec_out(1),
            vec, vec,
        ],
        out_specs=pl.BlockSpec(memory_space=pl.ANY),
        scratch_shapes=[pltpu.VMEM((MOE_TE, D_MODEL + META_W), F32), pltpu.VMEM((MOE_TE, D_MODEL + META_W), F32),
                        pltpu.VMEM((MOE_TE, D_MODEL), F32), pltpu.VMEM((MOE_TE, D_MODEL), F32),
                        pltpu.SemaphoreType.DMA((2,)), pltpu.SemaphoreType.DMA((2,))],
    )
    return pl.pallas_call(
        functools.partial(_moe_kernel, alpha=alpha, n_tokens=n_tokens),
        grid_spec=grid_spec,
        out_shape=jax.ShapeDtypeStruct((n_tokens + 2 * MOE_TE, D_MODEL), F32),
        compiler_params=_cparams(("arbitrary",)),
    )(tile_lo, tile_hi, ntiles, gsrc, gsrc, sdst, sdst, x1ext, wg, wu, wd, wg, wu, wd, g, b)


def _class_experts():
    lo, hi = [], []
    for grp in range(N_GROUPS):
        for a in range(EXPERTS_PER_GROUP):
            for c in range(a + 1, EXPERTS_PER_GROUP):
                lo.append(grp * EXPERTS_PER_GROUP + a)
                hi.append(grp * EXPERTS_PER_GROUP + c)
    return np.asarray(lo, np.int32), np.asarray(hi, np.int32)


def _routing_tables(x1ext, counts, n_tiles_max):
    t = x1ext.shape[0]
    cls = x1ext[:, D_MODEL].astype(jnp.int32)
    rank = x1ext[:, D_MODEL + 1].astype(jnp.int32)
    cnt = counts[0, :N_CLASSES].astype(jnp.int32)
    padded = ((cnt + MOE_TE - 1) // MOE_TE) * MOE_TE
    ends = jnp.cumsum(padded)
    offs = ends - padded
    dest = offs[cls] + rank
    pos = np.arange(n_tiles_max * MOE_TE)
    spare = jnp.asarray(t + ((pos // MOE_TE) % 2) * MOE_TE + pos % MOE_TE, dtype=jnp.int32)
    dst_rows = spare.at[dest].set(jnp.arange(t, dtype=jnp.int32))
    src_rows = jnp.where(dst_rows < t, dst_rows, 0)
    placeholder = jnp.asarray(t + MOE_TE + np.arange(MOE_TE), dtype=jnp.int32)
    sdst = jnp.concatenate([placeholder, dst_rows]).reshape(n_tiles_max + 1, 1, MOE_TE)
    gsrc = jnp.concatenate([src_rows, jnp.zeros((MOE_TE,), jnp.int32)]).reshape(n_tiles_max + 1, 1, MOE_TE)
    tile_start = jnp.arange(n_tiles_max, dtype=jnp.int32) * MOE_TE
    tile_cls = jnp.sum((ends[None, :] <= tile_start[:, None]).astype(jnp.int32), axis=1)
    tile_cls = jnp.minimum(tile_cls, N_CLASSES - 1)
    ntiles = (ends[-1] // MOE_TE).astype(jnp.int32).reshape(1)
    e_lo, e_hi = _class_experts()
    return jnp.asarray(e_lo)[tile_cls], jnp.asarray(e_hi)[tile_cls], ntiles, gsrc, sdst


def kernel(x, w_in, rel_bias, w_up_a, w_up_b, w_out, ln1_g, ln1_b, w_router, b_router,
           w_gate, w_up, w_down, ln2_g, ln2_b):
    batch, seq, d = x.shape
    depth = w_in.shape[0]
    assert d == D_MODEL and seq % ATT_TQ == 0 and seq >= ATT_WIN
    t = batch * seq
    assert t % PROJ_TM == 0 and t % MIX_TM == 0
    alpha = (2 * depth) ** 0.25
    n_tiles_max = t // MOE_TE + N_CLASSES

    scan_m = _sb_scan_matrix()
    tri = jnp.asarray(np.tril(np.ones((MIX_TM, MIX_TM), np.float32), -1), dtype=BF16)
    wr = jnp.zeros((D_MODEL, LANES), F32).at[:, :N_EXPERTS].set(w_router.astype(F32))
    wr = jnp.concatenate(_split_bf16(wr), axis=1)
    br = jnp.zeros((1, LANES), F32).at[0, :N_EXPERTS].set(b_router.astype(F32))

    bias_tiles = _attn_bias_tiles(rel_bias)

    x2d = x.reshape(t, d)
    for l in range(depth):
        proj = _inproj(x2d, w_in[l].astype(BF16), t)
        ya = _attn_a(proj, bias_tiles[l], batch, seq)
        yb = _attn_b(proj, scan_m, batch, seq)
        x1ext, counts = _mix(ya, yb, proj, x2d, w_up_a[l].astype(BF16), w_up_b[l].astype(BF16),
                             w_out[l].astype(BF16), ln1_g[l].reshape(1, d), ln1_b[l].reshape(1, d),
                             wr, br, tri, alpha, t)
        tile_lo, tile_hi, ntiles, gsrc, sdst = _routing_tables(x1ext, counts, n_tiles_max)
        x2d = _moe(x1ext, tile_lo, tile_hi, ntiles, gsrc, sdst,
                   w_gate[l].astype(BF16), w_up[l].astype(BF16), w_down[l].astype(BF16),
                   ln2_g[l].reshape(1, d), ln2_b[l].reshape(1, d), alpha, t)
    return x2d[:t].reshape(batch, seq, d)
```

```python
import functools

import numpy as np
import jax
import jax.numpy as jnp
from jax import lax
from jax.experimental import pallas as pl
from jax.experimental.pallas import tpu as pltpu

F32 = jnp.float32
BF16 = jnp.bfloat16

D_MODEL = 1024
HEAD_DIM = 64
N_HEADS = 8
WIDTH = N_HEADS * HEAD_DIM
CHUNK = 64
LEFT_CHUNKS = 8
MAX_REL = 128
N_EXPERTS = 16
N_GROUPS = 4
EXPERTS_PER_GROUP = 4
D_EXPERT = 512
LN_EPS = 1e-5
PROJ_COLS = 6 * WIDTH + 2 * D_MODEL
N_PAIRS = EXPERTS_PER_GROUP * (EXPERTS_PER_GROUP - 1) // 2
N_CLASSES = N_GROUPS * N_PAIRS

LANES = 128
HEADS_PER_LANE_BLOCK = LANES // HEAD_DIM
N_HEAD_PAIRS = N_HEADS // HEADS_PER_LANE_BLOCK
VMEM_LIMIT = 56 * 1024 * 1024

PROJ_TM = 1024
PROJ_TN = 512
ATT_TQ = 256
ATT_WIN = ATT_TQ + LEFT_CHUNKS * CHUNK
SB_Q = 256
SB_BLK = 128
SB_EXIT = -110.0
MIX_TM = 512
MOE_TE = 256
MOE_ROW_UNROLL = 8
META_W = LANES

NEG_BIG = -1e30


def _cparams(sem):
    return pltpu.CompilerParams(dimension_semantics=sem, vmem_limit_bytes=VMEM_LIMIT)


def _inproj_kernel(x_ref, w_ref, o_ref, *, scale):
    xb = x_ref[...].astype(BF16)
    n_chunks = PROJ_COLS // PROJ_TN
    q_chunks = (0, 3 * WIDTH // PROJ_TN)
    gate_start = 6 * WIDTH // PROJ_TN
    for c in range(n_chunks):
        cols = slice(c * PROJ_TN, (c + 1) * PROJ_TN)
        acc = jnp.dot(xb, w_ref[:, cols], preferred_element_type=F32)
        if c in q_chunks:
            acc = acc * scale
        elif c >= gate_start:
            acc = jax.nn.sigmoid(acc)
        o_ref[:, cols] = acc.astype(BF16)


def _inproj(x2d, w_bf16, t):
    assert PROJ_TN == WIDTH
    return pl.pallas_call(
        functools.partial(_inproj_kernel, scale=HEAD_DIM ** -0.5),
        grid=(t // PROJ_TM,),
        in_specs=[pl.BlockSpec((PROJ_TM, D_MODEL), lambda i: (i, 0)),
                  pl.BlockSpec((D_MODEL, PROJ_COLS), lambda i: (0, 0), pipeline_mode=pl.Buffered(1))],
        out_specs=pl.BlockSpec((PROJ_TM, PROJ_COLS), lambda i: (i, 0)),
        out_shape=jax.ShapeDtypeStruct((t, PROJ_COLS), BF16),
        compiler_params=_cparams(("parallel",)),
    )(x2d, w_bf16)


def _attn_bias_tiles(rel_bias):
    depth = rel_bias.shape[0]
    period = ATT_TQ + ATT_WIN
    m = np.arange(period)
    key_minus_query = np.where(m < ATT_WIN, m, m - period)
    cases = np.arange(3)[:, None] * ATT_TQ
    idx = np.clip(cases - key_minus_query[None, :], -MAX_REL, MAX_REL) + MAX_REL
    diag = jnp.take(rel_bias.astype(F32), jnp.asarray(idx.reshape(-1)), axis=-1)
    diag = diag.reshape(depth, N_HEADS, 3, period).transpose(0, 2, 1, 3)
    flat = jnp.tile(diag, (1, 1, 1, ATT_TQ))[..., :ATT_TQ * (period - 1)]
    tiles = flat.reshape(depth, 3, N_HEADS, ATT_TQ, period - 1)[..., :ATT_WIN]

    qpos = cases[:, :, None] + np.arange(ATT_TQ)[None, :, None]
    kpos = np.arange(ATT_WIN)[None, None, :]
    qc, kc = qpos // CHUNK, kpos // CHUNK
    allowed = (kc <= qc) & (kc >= qc - LEFT_CHUNKS)
    return jnp.where(jnp.asarray(allowed)[None, :, None], tiles, NEG_BIG)


def _attn_a_kernel(q_ref, k_ref, v_ref, bias_ref, o_ref):
    qt = pl.program_id(1)
    start = pl.multiple_of(jnp.maximum(qt * ATT_TQ - LEFT_CHUNKS * CHUNK, 0), ATT_TQ)
    first_head = lax.broadcasted_iota(jnp.int32, (1, LANES), 1) < HEAD_DIM
    for hp in range(N_HEAD_PAIRS):
        lanes = slice(hp * LANES, (hp + 1) * LANES)
        q = q_ref[:, lanes]
        kw = k_ref[pl.ds(start, ATT_WIN), lanes]
        vw = v_ref[pl.ds(start, ATT_WIN), lanes]
        outs = []
        for h in range(HEADS_PER_LANE_BLOCK):
            head_lanes = first_head if h == 0 else jnp.logical_not(first_head)
            qh = jnp.where(head_lanes, q, jnp.zeros_like(q))
            s = lax.dot_general(qh, kw, (((1,), (1,)), ((), ())), preferred_element_type=F32)
            s = s + bias_ref[0, hp * HEADS_PER_LANE_BLOCK + h]
            m = jnp.max(s, axis=-1, keepdims=True)
            p = jnp.exp(s - m)
            l = jnp.sum(p, axis=-1, keepdims=True)
            o = jnp.dot(p.astype(BF16), vw, preferred_element_type=F32)
            outs.append(o / l)
        o_ref[:, lanes] = jnp.where(first_head, outs[0], outs[1]).astype(BF16)


def _attn_a(proj, bias_tiles, batch, seq):
    n_qt = seq // ATT_TQ
    q_col, k_col, v_col = 0, 1, 2
    last_case = bias_tiles.shape[0] - 1
    return pl.pallas_call(
        _attn_a_kernel,
        grid=(batch, n_qt),
        in_specs=[
            pl.BlockSpec((ATT_TQ, WIDTH), lambda b, qt: (b * n_qt + qt, q_col)),
            pl.BlockSpec((seq, WIDTH), lambda b, qt: (b, k_col)),
            pl.BlockSpec((seq, WIDTH), lambda b, qt: (b, v_col)),
            pl.BlockSpec((1, N_HEADS, ATT_TQ, ATT_WIN), lambda b, qt: (jnp.minimum(qt, last_case), 0, 0, 0)),
        ],
        out_specs=pl.BlockSpec((ATT_TQ, WIDTH), lambda b, qt: (b * n_qt + qt, 0)),
        out_shape=jax.ShapeDtypeStruct((batch * seq, WIDTH), BF16),
        compiler_params=_cparams(("parallel", "arbitrary")),
    )(proj, proj, proj, bias_tiles)


def _sb_scan_matrix():
    jp = np.arange(SB_BLK)[:, None]
    c = np.arange(2 * SB_BLK)[None, :]
    half = np.where(c < SB_BLK, jp > c, True)
    return jnp.asarray(np.concatenate([half, half], axis=0), dtype=BF16)


def _sb_kernel(q_ref, k_ref, v_ref, m_ref, o_ref, acc_ref, carry_ref):
    qb = pl.program_id(1)
    first_head = lax.broadcasted_iota(jnp.int32, (1, LANES), 1) < HEAD_DIM

    acc_ref[...] = jnp.zeros_like(acc_ref)
    carry_ref[...] = jnp.zeros_like(carry_ref)

    def split_bf16(x):
        hi = lax.bitcast_convert_type(lax.bitcast_convert_type(x, jnp.int32) & jnp.int32(-65536), F32)
        return hi.astype(BF16), (x - hi).astype(BF16)

    def run_blocks(blocks):
        worst = None
        for hp in range(N_HEAD_PAIRS):
            lanes = slice(hp * LANES, (hp + 1) * LANES)
            staged = []
            for kb, row0, causal in blocks:
                n_rows = SB_Q - row0
                koff = pl.multiple_of(kb * SB_BLK, SB_BLK)
                q = q_ref[row0:, lanes]
                k = k_ref[pl.ds(koff, SB_BLK), lanes]
                v = v_ref[pl.ds(koff, SB_BLK), lanes]
                zero = jnp.zeros_like(k)
                k2 = jnp.concatenate([jnp.where(first_head, k, zero), jnp.where(first_head, zero, k)], axis=0)
                v2 = jnp.concatenate([jnp.where(first_head, v, zero), jnp.where(first_head, zero, v)], axis=0)
                z = lax.dot_general(q, k2, (((1,), (1,)), ((), ())), preferred_element_type=F32)
                log_beta = jnp.minimum(z, 0.0) - jnp.log(1.0 + jnp.exp(-jnp.abs(z)))
                log_keep = log_beta - z
                before2 = None
                if causal:
                    row = lax.broadcasted_iota(jnp.int32, (n_rows, 2 * SB_BLK), 0)
                    col = lax.broadcasted_iota(jnp.int32, (n_rows, 2 * SB_BLK), 1) & (SB_BLK - 1)
                    before2 = col < row
                    log_keep = jnp.where(before2, log_keep, 0.0)
                scans = []
                for h in range(HEADS_PER_LANE_BLOCK):
                    hi, lo = split_bf16(log_keep[:, h * SB_BLK:(h + 1) * SB_BLK])
                    scans.append(jnp.dot(jnp.concatenate([hi, lo], axis=1), m_ref[...], preferred_element_type=F32))
                staged.append((row0, log_beta, scans, before2, v2))

            carries = [carry_ref[hp * HEADS_PER_LANE_BLOCK + h] for h in range(HEADS_PER_LANE_BLOCK)]
            for row0, log_beta, scans, before2, v2 in staged:
                a_heads = []
                for h in range(HEADS_PER_LANE_BLOCK):
                    carry = carries[h]
                    a_heads.append(jnp.exp(log_beta[:, h * SB_BLK:(h + 1) * SB_BLK] + scans[h][:, :SB_BLK] + carry[row0:]))
                    moved = carry[row0:] + scans[h][:, SB_BLK:]
                    carries[h] = moved if row0 == 0 else jnp.concatenate([carry[:row0], moved], axis=0)
                a = jnp.concatenate(a_heads, axis=1)
                if before2 is not None:
                    a = jnp.where(before2, a, 0.0)
                acc_ref[hp, row0:, :] += jnp.dot(a.astype(BF16), v2, preferred_element_type=F32)
            for h in range(HEADS_PER_LANE_BLOCK):
                carry_ref[hp * HEADS_PER_LANE_BLOCK + h] = carries[h]
                worst = carries[h] if worst is None else jnp.maximum(worst, carries[h])
        return jnp.max(worst)

    blocks_per_tile = SB_Q // SB_BLK
    first_kb = qb * blocks_per_tile
    worst0 = run_blocks([(first_kb + d, d * SB_BLK, True) for d in reversed(range(blocks_per_tile))])

    def more(kb_min):
        return lambda state: jnp.logical_and(state[0] >= kb_min, state[1] >= SB_EXIT)

    def pair(state):
        kb, _ = state
        return kb - 2, run_blocks([(kb, 0, False), (kb - 1, 0, False)])

    def single(state):
        kb, _ = state
        return kb - 1, run_blocks([(kb, 0, False)])

    state = lax.while_loop(more(1), pair, (first_kb - 1, worst0))
    lax.while_loop(more(0), single, state)

    for hp in range(N_HEAD_PAIRS):
        o_ref[:, hp * LANES:(hp + 1) * LANES] = acc_ref[hp].astype(BF16)


def _attn_b(proj, scan_m, batch, seq):
    n_qb = seq // SB_Q
    q_col, k_col, v_col = 3, 4, 5
    return pl.pallas_call(
        _sb_kernel,
        grid=(batch, n_qb),
        in_specs=[
            pl.BlockSpec((SB_Q, WIDTH), lambda b, qb: (b * n_qb + qb, q_col)),
            pl.BlockSpec((seq, WIDTH), lambda b, qb: (b, k_col)),
            pl.BlockSpec((seq, WIDTH), lambda b, qb: (b, v_col)),
            pl.BlockSpec((2 * SB_BLK, 2 * SB_BLK), lambda b, qb: (0, 0)),
        ],
        out_specs=pl.BlockSpec((SB_Q, WIDTH), lambda b, qb: (b * n_qb + qb, 0)),
        out_shape=jax.ShapeDtypeStruct((batch * seq, WIDTH), BF16),
        scratch_shapes=[pltpu.VMEM((N_HEAD_PAIRS, SB_Q, LANES), F32),
                        pltpu.VMEM((N_HEADS, SB_Q, SB_BLK), F32)],
        compiler_params=_cparams(("parallel", "arbitrary")),
    )(proj, proj, proj, scan_m)


def _layer_norm(h, g, b):
    mu = jnp.mean(h, axis=-1, keepdims=True)
    d = h - mu
    var = jnp.mean(d * d, axis=-1, keepdims=True)
    return d * lax.rsqrt(var + LN_EPS) * g + b


def _split_bf16(x):
    hi = x.astype(BF16)
    return hi, (x - hi.astype(F32)).astype(BF16)


def _mix_kernel(ya_ref, yb_ref, ga_ref, gb_ref, x_ref, wa_ref, wb_ref, wo_ref, g_ref, b_ref,
                wr_ref, br_ref, tri_ref, o_ref, cnt_ref, route_ref, run_ref, *, alpha):
    @pl.when(pl.program_id(0) == 0)
    def _():
        run_ref[...] = jnp.zeros_like(run_ref)

    pa = jnp.dot(ya_ref[...], wa_ref[...], preferred_element_type=F32)
    pb = jnp.dot(yb_ref[...], wb_ref[...], preferred_element_type=F32)
    merged = ga_ref[...].astype(F32) * pa + gb_ref[...].astype(F32) * pb
    mixed = jnp.dot(merged.astype(BF16), wo_ref[...], preferred_element_type=F32)
    x1 = _layer_norm(alpha * x_ref[...] + mixed, g_ref[...], b_ref[...])
    o_ref[:, :D_MODEL] = x1

    xh, xl = _split_bf16(x1)
    rh = jnp.dot(xh, wr_ref[...], preferred_element_type=F32)
    rl = jnp.dot(xl, wr_ref[...], preferred_element_type=F32)
    logits = (rh[:, :LANES] + rl[:, :LANES]) + (rh[:, LANES:] + rl[:, LANES:]) + br_ref[...]
    lane = lax.broadcasted_iota(jnp.int32, logits.shape, 1)
    logits = jnp.where(lane < N_EXPERTS, logits, NEG_BIG)
    e = jnp.exp(logits - jnp.max(logits, axis=-1, keepdims=True))
    probs = e / jnp.sum(e, axis=-1, keepdims=True)
    p1 = jnp.max(probs, axis=-1, keepdims=True)
    i1 = jnp.min(jnp.where(probs == p1, lane, LANES), axis=-1, keepdims=True)
    group = i1 >> 2
    rest = jnp.where(jnp.logical_and(lane >> 2 == group, lane != i1), probs, -1.0)
    p2 = jnp.max(rest, axis=-1, keepdims=True)
    i2 = jnp.min(jnp.where(rest == p2, lane, LANES), axis=-1, keepdims=True)
    w1 = p1 / (p1 + p2)
    w2 = p2 / (p1 + p2)
    first_low = i1 < i2
    e_lo = jnp.where(first_low, i1, i2) & 3
    e_hi = jnp.where(first_low, i2, i1) & 3
    w_lo = jnp.where(first_low, w1, w2)
    w_hi = jnp.where(first_low, w2, w1)
    cls = group * N_PAIRS + ((e_lo * (5 - e_lo)) >> 1) + e_hi - 1

    onehot = (lane == cls).astype(F32)
    within = jnp.dot(tri_ref[...], onehot.astype(BF16), preferred_element_type=F32)
    rank = jnp.sum((within + run_ref[...]) * onehot, axis=-1, keepdims=True)
    run_ref[...] = run_ref[...] + jnp.sum(onehot, axis=0, keepdims=True)
    cnt_ref[...] = jnp.broadcast_to(run_ref[...], cnt_ref.shape)

    meta = jnp.where(lane == 0, cls.astype(F32),
                     jnp.where(lane == 1, rank,
                               jnp.where(lane == 2, w_lo, jnp.where(lane == 3, w_hi, 0.0))))
    o_ref[:, D_MODEL:] = meta
    route_ref[...] = meta.T[:8]


def _mix(ya, yb, proj, x2d, wa, wb, wo, g, b, wr, br, tri, alpha, t):
    ga_col = 6 * WIDTH // D_MODEL
    const = lambda shape: pl.BlockSpec(shape, lambda i: (0,) * len(shape))
    return pl.pallas_call(
        functools.partial(_mix_kernel, alpha=alpha),
        grid=(t // MIX_TM,),
        in_specs=[
            pl.BlockSpec((MIX_TM, WIDTH), lambda i: (i, 0)),
            pl.BlockSpec((MIX_TM, WIDTH), lambda i: (i, 0)),
            pl.BlockSpec((MIX_TM, D_MODEL), lambda i: (i, ga_col)),
            pl.BlockSpec((MIX_TM, D_MODEL), lambda i: (i, ga_col + 1)),
            pl.BlockSpec((MIX_TM, D_MODEL), lambda i: (i, 0)),
            const((WIDTH, D_MODEL)), const((WIDTH, D_MODEL)), const((D_MODEL, D_MODEL)),
            const((1, D_MODEL)), const((1, D_MODEL)),
            const((D_MODEL, 2 * LANES)), const((1, LANES)),
            const((MIX_TM, MIX_TM)),
        ],
        out_specs=[pl.BlockSpec((MIX_TM, D_MODEL + META_W), lambda i: (i, 0)),
                   pl.BlockSpec((8, LANES), lambda i: (0, 0)),
                   pl.BlockSpec((8, MIX_TM), lambda i: (0, i))],
        out_shape=[jax.ShapeDtypeStruct((t, D_MODEL + META_W), F32),
                   jax.ShapeDtypeStruct((8, LANES), F32),
                   jax.ShapeDtypeStruct((8, t), F32)],
        scratch_shapes=[pltpu.VMEM((1, LANES), F32)],
        compiler_params=_cparams(("arbitrary",)),
    )(ya, yb, proj, proj, x2d, wa, wb, wo, g, b, wr, br, tri)


def _moe_kernel(tlo_ref, thi_ref, ntiles_ref,
                gcur_ref, gnxt_ref, dprev_ref, dcur_ref, x_hbm,
                wg0_ref, wu0_ref, wd0_ref, wg1_ref, wu1_ref, wd1_ref, g_ref, b_ref,
                out_hbm, xg0_ref, xg1_ref, y0_ref, y1_ref, sem_in, sem_out, *, alpha, n_tokens):
    i = pl.program_id(0)
    ntiles = ntiles_ref[0]
    xg_refs = (xg0_ref, xg1_ref)
    y_refs = (y0_ref, y1_ref)

    def looped_rows(fn):
        def body(c, carry):
            base = pl.multiple_of(c * MOE_ROW_UNROLL, MOE_ROW_UNROLL)
            for u in range(MOE_ROW_UNROLL):
                fn(base + u)
            return carry
        lax.fori_loop(0, MOE_TE // MOE_ROW_UNROLL, body, 0)

    def gather_copy(idx_ref, s, r):
        return pltpu.make_async_copy(x_hbm.at[pl.ds(idx_ref[0, 0, r], 1)], xg_refs[s].at[pl.ds(r, 1)], sem_in.at[s])

    def scatter_copy(idx_ref, s, r):
        return pltpu.make_async_copy(y_refs[s].at[pl.ds(r, 1)], out_hbm.at[pl.ds(idx_ref[0, 0, r], 1)], sem_out.at[s])

    @pl.when(i == 0)
    def _():
        y1_ref[...] = jnp.zeros_like(y1_ref)
        fill = pltpu.make_async_copy(y1_ref, out_hbm.at[pl.ds(n_tokens, MOE_TE)], sem_out.at[1])
        fill.start()
        fill.wait()
        looped_rows(lambda r: gather_copy(gcur_ref, 0, r).start())

    def step(p):
        q = 1 - p
        xg_ref, y_ref = xg_refs[p], y_refs[p]
        looped_rows(lambda r: gather_copy(gcur_ref, p, r).wait())

        @pl.when(i >= 1)
        def _():
            looped_rows(lambda r: scatter_copy(dcur_ref, p, r).wait())

        n_groups = 8
        group_rows = MOE_TE // n_groups
        groups = iter(range(n_groups))

        def issue_group():
            base = next(groups) * group_rows
            for r in range(base, base + group_rows):
                gather_copy(gnxt_ref, q, r).start(priority=1)
                scatter_copy(dprev_ref, q, r).start(priority=r % 2)

        xg = xg_ref[...]
        x = xg[:, :D_MODEL]
        xb = x.astype(BF16)
        w_lo = xg[:, D_MODEL + 2:D_MODEL + 3]
        w_hi = xg[:, D_MODEL + 3:D_MODEL + 4]

        def expert(wg_ref, wu_ref, wd_ref):
            issue_group()
            gate = jnp.dot(xb, wg_ref[0], preferred_element_type=F32)
            issue_group()
            up = jnp.dot(xb, wu_ref[0], preferred_element_type=F32)
            issue_group()
            hidden = (gate * jax.nn.sigmoid(gate) * up).astype(BF16)
            issue_group()
            return jnp.dot(hidden, wd_ref[0], preferred_element_type=F32)

        ffn = w_lo * expert(wg0_ref, wu0_ref, wd0_ref) + w_hi * expert(wg1_ref, wu1_ref, wd1_ref)
        y_ref[...] = _layer_norm(alpha * x + ffn, g_ref[...], b_ref[...])

        @pl.when(i == ntiles - 1)
        def _():
            looped_rows(lambda r: gather_copy(gnxt_ref, q, r).wait())
            looped_rows(lambda r: scatter_copy(dcur_ref, p, r).start())
            looped_rows(lambda r: scatter_copy(dprev_ref, q, r).wait())
            looped_rows(lambda r: scatter_copy(dcur_ref, p, r).wait())

    @pl.when(i < ntiles)
    def _():
        parity = lax.rem(i, 2)
        for p in range(2):
            pl.when(parity == p)(functools.partial(step, p))


def _moe(x1ext, tile_lo, tile_hi, ntiles, gsrc, sdst, wg, wu, wd, g, b, alpha, n_tokens):
    n_tiles_max = gsrc.shape[0] - 1
    wspec_in = lambda which: pl.BlockSpec(
        (1, D_MODEL, D_EXPERT), lambda i, tlo, thi, nt: ((tlo, thi)[which][i], 0, 0))
    wspec_out = lambda which: pl.BlockSpec(
        (1, D_EXPERT, D_MODEL), lambda i, tlo, thi, nt: ((tlo, thi)[which][i], 0, 0))
    vec = pl.BlockSpec((1, D_MODEL), lambda i, *_: (0, 0))
    rows = lambda shift: pl.BlockSpec((1, 1, MOE_TE), lambda i, *_: (i + shift, 0, 0), memory_space=pltpu.SMEM)
    grid_spec = pltpu.PrefetchScalarGridSpec(
        num_scalar_prefetch=3,
        grid=(n_tiles_max,),
        in_specs=[
            rows(0), rows(1),
            rows(0), rows(1),
            pl.BlockSpec(memory_space=pl.ANY),
            wspec_in(0), wspec_in(0), wspec_out(0), wspec_in(1), wspec_in(1), wspec_out(1),
            vec, vec,
        ],
        out_specs=pl.BlockSpec(memory_space=pl.ANY),
        scratch_shapes=[pltpu.VMEM((MOE_TE, D_MODEL + META_W), F32), pltpu.VMEM((MOE_TE, D_MODEL + META_W), F32),
                        pltpu.VMEM((MOE_TE, D_MODEL), F32), pltpu.VMEM((MOE_TE, D_MODEL), F32),
                        pltpu.SemaphoreType.DMA((2,)), pltpu.SemaphoreType.DMA((2,))],
    )
    return pl.pallas_call(
        functools.partial(_moe_kernel, alpha=alpha, n_tokens=n_tokens),
        grid_spec=grid_spec,
        out_shape=jax.ShapeDtypeStruct((n_tokens + 2 * MOE_TE, D_MODEL), F32),
        compiler_params=_cparams(("arbitrary",)),
    )(tile_lo, tile_hi, ntiles, gsrc, gsrc, sdst, sdst, x1ext, wg, wu, wd, wg, wu, wd, g, b)


def _class_experts():
    lo, hi = [], []
    for grp in range(N_GROUPS):
        for a in range(EXPERTS_PER_GROUP):
            for c in range(a + 1, EXPERTS_PER_GROUP):
                lo.append(grp * EXPERTS_PER_GROUP + a)
                hi.append(grp * EXPERTS_PER_GROUP + c)
    return np.asarray(lo, np.int32), np.asarray(hi, np.int32)


def _routing_tables(route, counts, n_tiles_max):
    t = route.shape[1]
    cls = route[0].astype(jnp.int32)
    rank = route[1].astype(jnp.int32)
    cnt = counts[0, :N_CLASSES].astype(jnp.int32)
    padded = ((cnt + MOE_TE - 1) // MOE_TE) * MOE_TE
    ends = jnp.cumsum(padded)
    offs = ends - padded
    dest = offs[cls] + rank
    pos = np.arange(n_tiles_max * MOE_TE)
    spare = jnp.asarray(t + ((pos // MOE_TE) % 2) * MOE_TE + pos % MOE_TE, dtype=jnp.int32)
    dst_rows = spare.at[dest].set(jnp.arange(t, dtype=jnp.int32), unique_indices=True)
    src_rows = jnp.where(dst_rows < t, dst_rows, 0)
    placeholder = jnp.asarray(t + MOE_TE + np.arange(MOE_TE), dtype=jnp.int32)
    sdst = jnp.concatenate([placeholder, dst_rows]).reshape(n_tiles_max + 1, 1, MOE_TE)
    gsrc = jnp.concatenate([src_rows, jnp.zeros((MOE_TE,), jnp.int32)]).reshape(n_tiles_max + 1, 1, MOE_TE)
    tile_start = jnp.arange(n_tiles_max, dtype=jnp.int32) * MOE_TE
    tile_cls = jnp.sum((ends[None, :] <= tile_start[:, None]).astype(jnp.int32), axis=1)
    tile_cls = jnp.minimum(tile_cls, N_CLASSES - 1)
    ntiles = (ends[-1] // MOE_TE).astype(jnp.int32).reshape(1)
    e_lo, e_hi = _class_experts()
    return jnp.asarray(e_lo)[tile_cls], jnp.asarray(e_hi)[tile_cls], ntiles, gsrc, sdst


def kernel(x, w_in, rel_bias, w_up_a, w_up_b, w_out, ln1_g, ln1_b, w_router, b_router,
           w_gate, w_up, w_down, ln2_g, ln2_b):
    batch, seq, d = x.shape
    depth = w_in.shape[0]
    assert d == D_MODEL and seq % ATT_TQ == 0 and seq >= ATT_WIN
    t = batch * seq
    assert t % PROJ_TM == 0 and t % MIX_TM == 0
    alpha = (2 * depth) ** 0.25
    n_tiles_max = t // MOE_TE + N_CLASSES

    scan_m = _sb_scan_matrix()
    tri = jnp.asarray(np.tril(np.ones((MIX_TM, MIX_TM), np.float32), -1), dtype=BF16)
    wr = jnp.zeros((D_MODEL, LANES), F32).at[:, :N_EXPERTS].set(w_router.astype(F32))
    wr = jnp.concatenate(_split_bf16(wr), axis=1)
    br = jnp.zeros((1, LANES), F32).at[0, :N_EXPERTS].set(b_router.astype(F32))

    bias_tiles = _attn_bias_tiles(rel_bias)

    x2d = x.reshape(t, d)
    for l in range(depth):
        proj = _inproj(x2d, w_in[l].astype(BF16), t)
        ya = _attn_a(proj, bias_tiles[l], batch, seq)
        yb = _attn_b(proj, scan_m, batch, seq)
        x1ext, counts, route = _mix(ya, yb, proj, x2d, w_up_a[l].astype(BF16), w_up_b[l].astype(BF16),
                                    w_out[l].astype(BF16), ln1_g[l].reshape(1, d), ln1_b[l].reshape(1, d),
                                    wr, br, tri, alpha, t)
        tile_lo, tile_hi, ntiles, gsrc, sdst = _routing_tables(route, counts, n_tiles_max)
        x2d = _moe(x1ext, tile_lo, tile_hi, ntiles, gsrc, sdst,
                   w_gate[l].astype(BF16), w_up[l].astype(BF16), w_down[l].astype(BF16),
                   ln2_g[l].reshape(1, d), ln2_b[l].reshape(1, d), alpha, t)
    return x2d[:t].reshape(batch, seq, d)
```

```python
import functools

import numpy as np
import jax
import jax.numpy as jnp
from jax import lax
from jax.experimental import pallas as pl
from jax.experimental.pallas import tpu as pltpu

F32 = jnp.float32
BF16 = jnp.bfloat16

D_MODEL = 1024
HEAD_DIM = 64
N_HEADS = 8
WIDTH = N_HEADS * HEAD_DIM
CHUNK = 64
LEFT_CHUNKS = 8
MAX_REL = 128
N_EXPERTS = 16
N_GROUPS = 4
EXPERTS_PER_GROUP = 4
D_EXPERT = 512
LN_EPS = 1e-5
PROJ_COLS = 6 * WIDTH + 2 * D_MODEL
N_PAIRS = EXPERTS_PER_GROUP * (EXPERTS_PER_GROUP - 1) // 2
N_CLASSES = N_GROUPS * N_PAIRS

LANES = 128
HEADS_PER_LANE_BLOCK = LANES // HEAD_DIM
N_HEAD_PAIRS = N_HEADS // HEADS_PER_LANE_BLOCK
VMEM_LIMIT = 56 * 1024 * 1024

PROJ_TM = 1024
PROJ_TN = 512
ATT_TQ = 256
ATT_WIN = ATT_TQ + LEFT_CHUNKS * CHUNK
SB_Q = 256
SB_BLK = 128
SB_EXIT = -110.0
MIX_TM = 512
MOE_TE = 256
MOE_ROW_UNROLL = 8
META_W = LANES

NEG_BIG = -1e30

ROW_TILES = D_MODEL // LANES + 1


def _cparams(sem):
    return pltpu.CompilerParams(dimension_semantics=sem, vmem_limit_bytes=VMEM_LIMIT)


def _load_token_rows(ref, n_rows):
    return jnp.concatenate([ref[pl.ds(c, n_rows, stride=ROW_TILES), :] for c in range(D_MODEL // LANES)], axis=1)


def _store_token_rows(ref, n_rows, values, extra):
    for c in range(D_MODEL // LANES):
        ref[pl.ds(c, n_rows, stride=ROW_TILES), :] = values[:, c * LANES:(c + 1) * LANES]
    ref[pl.ds(D_MODEL // LANES, n_rows, stride=ROW_TILES), :] = extra


def _inproj_kernel(x_ref, w_ref, o_ref, *, scale):
    xb = _load_token_rows(x_ref, PROJ_TM).astype(BF16)
    n_chunks = PROJ_COLS // PROJ_TN
    q_chunks = (0, 3 * WIDTH // PROJ_TN)
    gate_start = 6 * WIDTH // PROJ_TN
    for c in range(n_chunks):
        cols = slice(c * PROJ_TN, (c + 1) * PROJ_TN)
        acc = jnp.dot(xb, w_ref[:, cols], preferred_element_type=F32)
        if c in q_chunks:
            acc = acc * scale
        elif c >= gate_start:
            acc = jax.nn.sigmoid(acc)
        o_ref[:, cols] = acc.astype(BF16)


def _inproj(x2d, w_bf16, t):
    assert PROJ_TN == WIDTH
    return pl.pallas_call(
        functools.partial(_inproj_kernel, scale=HEAD_DIM ** -0.5),
        grid=(t // PROJ_TM,),
        in_specs=[pl.BlockSpec((PROJ_TM * ROW_TILES, LANES), lambda i: (i, 0)),
                  pl.BlockSpec((D_MODEL, PROJ_COLS), lambda i: (0, 0), pipeline_mode=pl.Buffered(1))],
        out_specs=pl.BlockSpec((PROJ_TM, PROJ_COLS), lambda i: (i, 0)),
        out_shape=jax.ShapeDtypeStruct((t, PROJ_COLS), BF16),
        compiler_params=_cparams(("parallel",)),
    )(x2d, w_bf16)


def _attn_bias_tiles(rel_bias):
    depth = rel_bias.shape[0]
    period = ATT_TQ + ATT_WIN
    m = np.arange(period)
    key_minus_query = np.where(m < ATT_WIN, m, m - period)
    cases = np.arange(3)[:, None] * ATT_TQ
    idx = np.clip(cases - key_minus_query[None, :], -MAX_REL, MAX_REL) + MAX_REL
    diag = jnp.take(rel_bias.astype(F32), jnp.asarray(idx.reshape(-1)), axis=-1)
    diag = diag.reshape(depth, N_HEADS, 3, period).transpose(0, 2, 1, 3)
    flat = jnp.tile(diag, (1, 1, 1, ATT_TQ))[..., :ATT_TQ * (period - 1)]
    tiles = flat.reshape(depth, 3, N_HEADS, ATT_TQ, period - 1)[..., :ATT_WIN]

    qpos = cases[:, :, None] + np.arange(ATT_TQ)[None, :, None]
    kpos = np.arange(ATT_WIN)[None, None, :]
    qc, kc = qpos // CHUNK, kpos // CHUNK
    allowed = (kc <= qc) & (kc >= qc - LEFT_CHUNKS)
    return jnp.where(jnp.asarray(allowed)[None, :, None], tiles, NEG_BIG)


def _attn_a_kernel(q_ref, k_ref, v_ref, bias_ref, o_ref):
    qt = pl.program_id(1)
    start = pl.multiple_of(jnp.maximum(qt * ATT_TQ - LEFT_CHUNKS * CHUNK, 0), ATT_TQ)
    first_head = lax.broadcasted_iota(jnp.int32, (1, LANES), 1) < HEAD_DIM
    for hp in range(N_HEAD_PAIRS):
        lanes = slice(hp * LANES, (hp + 1) * LANES)
        q = q_ref[:, lanes]
        kw = k_ref[pl.ds(start, ATT_WIN), lanes]
        vw = v_ref[pl.ds(start, ATT_WIN), lanes]
        outs = []
        for h in range(HEADS_PER_LANE_BLOCK):
            head_lanes = first_head if h == 0 else jnp.logical_not(first_head)
            qh = jnp.where(head_lanes, q, jnp.zeros_like(q))
            s = lax.dot_general(qh, kw, (((1,), (1,)), ((), ())), preferred_element_type=F32)
            s = s + bias_ref[0, hp * HEADS_PER_LANE_BLOCK + h]
            m = jnp.max(s, axis=-1, keepdims=True)
            p = jnp.exp(s - m)
            l = jnp.sum(p, axis=-1, keepdims=True)
            o = jnp.dot(p.astype(BF16), vw, preferred_element_type=F32)
            outs.append(o / l)
        o_ref[:, lanes] = jnp.where(first_head, outs[0], outs[1]).astype(BF16)


def _attn_a(proj, bias_tiles, batch, seq):
    n_qt = seq // ATT_TQ
    q_col, k_col, v_col = 0, 1, 2
    last_case = bias_tiles.shape[0] - 1
    return pl.pallas_call(
        _attn_a_kernel,
        grid=(batch, n_qt),
        in_specs=[
            pl.BlockSpec((ATT_TQ, WIDTH), lambda b, qt: (b * n_qt + qt, q_col)),
            pl.BlockSpec((seq, WIDTH), lambda b, qt: (b, k_col)),
            pl.BlockSpec((seq, WIDTH), lambda b, qt: (b, v_col)),
            pl.BlockSpec((1, N_HEADS, ATT_TQ, ATT_WIN), lambda b, qt: (jnp.minimum(qt, last_case), 0, 0, 0)),
        ],
        out_specs=pl.BlockSpec((ATT_TQ, WIDTH), lambda b, qt: (b * n_qt + qt, 0)),
        out_shape=jax.ShapeDtypeStruct((batch * seq, WIDTH), BF16),
        compiler_params=_cparams(("parallel", "arbitrary")),
    )(proj, proj, proj, bias_tiles)


def _sb_scan_matrix():
    jp = np.arange(SB_BLK)[:, None]
    c = np.arange(2 * SB_BLK)[None, :]
    half = np.where(c < SB_BLK, jp > c, True)
    return jnp.asarray(np.concatenate([half, half], axis=0), dtype=BF16)


def _sb_kernel(q_ref, k_ref, v_ref, m_ref, o_ref, acc_ref, carry_ref):
    qb = pl.program_id(1)
    first_head = lax.broadcasted_iota(jnp.int32, (1, LANES), 1) < HEAD_DIM

    acc_ref[...] = jnp.zeros_like(acc_ref)
    carry_ref[...] = jnp.zeros_like(carry_ref)

    def split_bf16(x):
        hi = lax.bitcast_convert_type(lax.bitcast_convert_type(x, jnp.int32) & jnp.int32(-65536), F32)
        return hi.astype(BF16), (x - hi).astype(BF16)

    def run_blocks(blocks):
        worst = None
        for hp in range(N_HEAD_PAIRS):
            lanes = slice(hp * LANES, (hp + 1) * LANES)
            staged = []
            for kb, row0, causal in blocks:
                n_rows = SB_Q - row0
                koff = pl.multiple_of(kb * SB_BLK, SB_BLK)
                q = q_ref[row0:, lanes]
                k = k_ref[pl.ds(koff, SB_BLK), lanes]
                v = v_ref[pl.ds(koff, SB_BLK), lanes]
                zero = jnp.zeros_like(k)
                k2 = jnp.concatenate([jnp.where(first_head, k, zero), jnp.where(first_head, zero, k)], axis=0)
                v2 = jnp.concatenate([jnp.where(first_head, v, zero), jnp.where(first_head, zero, v)], axis=0)
                z = lax.dot_general(q, k2, (((1,), (1,)), ((), ())), preferred_element_type=F32)
                log_beta = jnp.minimum(z, 0.0) - jnp.log(1.0 + jnp.exp(-jnp.abs(z)))
                log_keep = log_beta - z
                before2 = None
                if causal:
                    row = lax.broadcasted_iota(jnp.int32, (n_rows, 2 * SB_BLK), 0)
                    col = lax.broadcasted_iota(jnp.int32, (n_rows, 2 * SB_BLK), 1) & (SB_BLK - 1)
                    before2 = col < row
                    log_keep = jnp.where(before2, log_keep, 0.0)
                scans = []
                for h in range(HEADS_PER_LANE_BLOCK):
                    hi, lo = split_bf16(log_keep[:, h * SB_BLK:(h + 1) * SB_BLK])
                    scans.append(jnp.dot(jnp.concatenate([hi, lo], axis=1), m_ref[...], preferred_element_type=F32))
                staged.append((row0, log_beta, scans, before2, v2))

            carries = [carry_ref[hp * HEADS_PER_LANE_BLOCK + h] for h in range(HEADS_PER_LANE_BLOCK)]
            for row0, log_beta, scans, before2, v2 in staged:
                a_heads = []
                for h in range(HEADS_PER_LANE_BLOCK):
                    carry = carries[h]
                    a_heads.append(jnp.exp(log_beta[:, h * SB_BLK:(h + 1) * SB_BLK] + scans[h][:, :SB_BLK] + carry[row0:]))
                    moved = carry[row0:] + scans[h][:, SB_BLK:]
                    carries[h] = moved if row0 == 0 else jnp.concatenate([carry[:row0], moved], axis=0)
                a = jnp.concatenate(a_heads, axis=1)
                if before2 is not None:
                    a = jnp.where(before2, a, 0.0)
                acc_ref[hp, row0:, :] += jnp.dot(a.astype(BF16), v2, preferred_element_type=F32)
            for h in range(HEADS_PER_LANE_BLOCK):
                carry_ref[hp * HEADS_PER_LANE_BLOCK + h] = carries[h]
                worst = carries[h] if worst is None else jnp.maximum(worst, carries[h])
        return jnp.max(worst)

    blocks_per_tile = SB_Q // SB_BLK
    first_kb = qb * blocks_per_tile
    worst0 = run_blocks([(first_kb + d, d * SB_BLK, True) for d in reversed(range(blocks_per_tile))])

    def more(kb_min):
        return lambda state: jnp.logical_and(state[0] >= kb_min, state[1] >= SB_EXIT)

    def pair(state):
        kb, _ = state
        return kb - 2, run_blocks([(kb, 0, False), (kb - 1, 0, False)])

    def single(state):
        kb, _ = state
        return kb - 1, run_blocks([(kb, 0, False)])

    state = lax.while_loop(more(1), pair, (first_kb - 1, worst0))
    lax.while_loop(more(0), single, state)

    for hp in range(N_HEAD_PAIRS):
        o_ref[:, hp * LANES:(hp + 1) * LANES] = acc_ref[hp].astype(BF16)


def _attn_b(proj, scan_m, batch, seq):
    n_qb = seq // SB_Q
    q_col, k_col, v_col = 3, 4, 5
    return pl.pallas_call(
        _sb_kernel,
        grid=(batch, n_qb),
        in_specs=[
            pl.BlockSpec((SB_Q, WIDTH), lambda b, qb: (b * n_qb + qb, q_col)),
            pl.BlockSpec((seq, WIDTH), lambda b, qb: (b, k_col)),
            pl.BlockSpec((seq, WIDTH), lambda b, qb: (b, v_col)),
            pl.BlockSpec((2 * SB_BLK, 2 * SB_BLK), lambda b, qb: (0, 0)),
        ],
        out_specs=pl.BlockSpec((SB_Q, WIDTH), lambda b, qb: (b * n_qb + qb, 0)),
        out_shape=jax.ShapeDtypeStruct((batch * seq, WIDTH), BF16),
        scratch_shapes=[pltpu.VMEM((N_HEAD_PAIRS, SB_Q, LANES), F32),
                        pltpu.VMEM((N_HEADS, SB_Q, SB_BLK), F32)],
        compiler_params=_cparams(("parallel", "arbitrary")),
    )(proj, proj, proj, scan_m)


def _layer_norm(h, g, b):
    mu = jnp.mean(h, axis=-1, keepdims=True)
    d = h - mu
    var = jnp.mean(d * d, axis=-1, keepdims=True)
    return d * lax.rsqrt(var + LN_EPS) * g + b


def _split_bf16(x):
    hi = x.astype(BF16)
    return hi, (x - hi.astype(F32)).astype(BF16)


def _mix_kernel(ya_ref, yb_ref, ga_ref, gb_ref, x_ref, wa_ref, wb_ref, wo_ref, g_ref, b_ref,
                wr_ref, br_ref, tri_ref, o_ref, cnt_ref, route_ref, run_ref, *, alpha):
    @pl.when(pl.program_id(0) == 0)
    def _():
        run_ref[...] = jnp.zeros_like(run_ref)

    pa = jnp.dot(ya_ref[...], wa_ref[...], preferred_element_type=F32)
    pb = jnp.dot(yb_ref[...], wb_ref[...], preferred_element_type=F32)
    merged = ga_ref[...].astype(F32) * pa + gb_ref[...].astype(F32) * pb
    mixed = jnp.dot(merged.astype(BF16), wo_ref[...], preferred_element_type=F32)
    x1 = _layer_norm(alpha * _load_token_rows(x_ref, MIX_TM) + mixed, g_ref[...], b_ref[...])

    xh, xl = _split_bf16(x1)
    rh = jnp.dot(xh, wr_ref[...], preferred_element_type=F32)
    rl = jnp.dot(xl, wr_ref[...], preferred_element_type=F32)
    logits = (rh[:, :LANES] + rl[:, :LANES]) + (rh[:, LANES:] + rl[:, LANES:]) + br_ref[...]
    lane = lax.broadcasted_iota(jnp.int32, logits.shape, 1)
    logits = jnp.where(lane < N_EXPERTS, logits, NEG_BIG)
    e = jnp.exp(logits - jnp.max(logits, axis=-1, keepdims=True))
    probs = e / jnp.sum(e, axis=-1, keepdims=True)
    p1 = jnp.max(probs, axis=-1, keepdims=True)
    i1 = jnp.min(jnp.where(probs == p1, lane, LANES), axis=-1, keepdims=True)
    group = i1 >> 2
    rest = jnp.where(jnp.logical_and(lane >> 2 == group, lane != i1), probs, -1.0)
    p2 = jnp.max(rest, axis=-1, keepdims=True)
    i2 = jnp.min(jnp.where(rest == p2, lane, LANES), axis=-1, keepdims=True)
    w1 = p1 / (p1 + p2)
    w2 = p2 / (p1 + p2)
    first_low = i1 < i2
    e_lo = jnp.where(first_low, i1, i2) & 3
    e_hi = jnp.where(first_low, i2, i1) & 3
    w_lo = jnp.where(first_low, w1, w2)
    w_hi = jnp.where(first_low, w2, w1)
    cls = group * N_PAIRS + ((e_lo * (5 - e_lo)) >> 1) + e_hi - 1

    onehot = (lane == cls).astype(F32)
    within = jnp.dot(tri_ref[...], onehot.astype(BF16), preferred_element_type=F32)
    rank = jnp.sum((within + run_ref[...]) * onehot, axis=-1, keepdims=True)
    run_ref[...] = run_ref[...] + jnp.sum(onehot, axis=0, keepdims=True)
    cnt_ref[...] = jnp.broadcast_to(run_ref[...], cnt_ref.shape)

    meta = jnp.where(lane == 0, cls.astype(F32),
                     jnp.where(lane == 1, rank,
                               jnp.where(lane == 2, w_lo, jnp.where(lane == 3, w_hi, 0.0))))
    _store_token_rows(o_ref, MIX_TM, x1, meta)
    route_ref[...] = meta.T[:8]


def _mix(ya, yb, proj, x2d, wa, wb, wo, g, b, wr, br, tri, alpha, t):
    ga_col = 6 * WIDTH // D_MODEL
    const = lambda shape: pl.BlockSpec(shape, lambda i: (0,) * len(shape))
    return pl.pallas_call(
        functools.partial(_mix_kernel, alpha=alpha),
        grid=(t // MIX_TM,),
        in_specs=[
            pl.BlockSpec((MIX_TM, WIDTH), lambda i: (i, 0)),
            pl.BlockSpec((MIX_TM, WIDTH), lambda i: (i, 0)),
            pl.BlockSpec((MIX_TM, D_MODEL), lambda i: (i, ga_col)),
            pl.BlockSpec((MIX_TM, D_MODEL), lambda i: (i, ga_col + 1)),
            pl.BlockSpec((MIX_TM * ROW_TILES, LANES), lambda i: (i, 0)),
            const((WIDTH, D_MODEL)), const((WIDTH, D_MODEL)), const((D_MODEL, D_MODEL)),
            const((1, D_MODEL)), const((1, D_MODEL)),
            const((D_MODEL, 2 * LANES)), const((1, LANES)),
            const((MIX_TM, MIX_TM)),
        ],
        out_specs=[pl.BlockSpec((MIX_TM * ROW_TILES, LANES), lambda i: (i, 0)),
                   pl.BlockSpec((8, LANES), lambda i: (0, 0)),
                   pl.BlockSpec((8, MIX_TM), lambda i: (0, i))],
        out_shape=[jax.ShapeDtypeStruct((t * ROW_TILES, LANES), F32),
                   jax.ShapeDtypeStruct((8, LANES), F32),
                   jax.ShapeDtypeStruct((8, t), F32)],
        scratch_shapes=[pltpu.VMEM((1, LANES), F32)],
        compiler_params=_cparams(("arbitrary",)),
    )(ya, yb, proj, proj, x2d, wa, wb, wo, g, b, wr, br, tri)


def _moe_kernel(tlo_ref, thi_ref, ntiles_ref,
                gcur_ref, gnxt_ref, dprev_ref, dcur_ref, x_hbm,
                wg0_ref, wu0_ref, wd0_ref, wg1_ref, wu1_ref, wd1_ref, g_ref, b_ref,
                out_hbm, xg0_ref, xg1_ref, y0_ref, y1_ref, sem_in, sem_out, *, alpha, n_tokens):
    i = pl.program_id(0)
    ntiles = ntiles_ref[0]
    xg_refs = (xg0_ref, xg1_ref)
    y_refs = (y0_ref, y1_ref)

    def looped_rows(fn):
        def body(c, carry):
            base = pl.multiple_of(c * MOE_ROW_UNROLL, MOE_ROW_UNROLL)
            for u in range(MOE_ROW_UNROLL):
                fn(base + u)
            return carry
        lax.fori_loop(0, MOE_TE // MOE_ROW_UNROLL, body, 0)

    def gather_copy(idx_ref, s, r):
        return pltpu.make_async_copy(x_hbm.at[pl.ds(idx_ref[0, 0, r], ROW_TILES)],
                                     xg_refs[s].at[pl.ds(r * ROW_TILES, ROW_TILES)], sem_in.at[s])

    def scatter_copy(idx_ref, s, r):
        return pltpu.make_async_copy(y_refs[s].at[pl.ds(r * ROW_TILES, ROW_TILES)],
                                     out_hbm.at[pl.ds(idx_ref[0, 0, r], ROW_TILES)], sem_out.at[s])

    @pl.when(i == 0)
    def _():
        y1_ref[...] = jnp.zeros_like(y1_ref)
        fill = pltpu.make_async_copy(y1_ref, out_hbm.at[pl.ds(n_tokens * ROW_TILES, MOE_TE * ROW_TILES)],
                                     sem_out.at[1])
        fill.start()
        fill.wait()
        looped_rows(lambda r: gather_copy(gcur_ref, 0, r).start())

    def step(p):
        q = 1 - p
        xg_ref, y_ref = xg_refs[p], y_refs[p]
        looped_rows(lambda r: gather_copy(gcur_ref, p, r).wait())

        @pl.when(i >= 1)
        def _():
            looped_rows(lambda r: scatter_copy(dcur_ref, p, r).wait())

        n_groups = 8
        group_rows = MOE_TE // n_groups
        groups = iter(range(n_groups))

        def issue_group():
            base = next(groups) * group_rows
            for r in range(base, base + group_rows):
                gather_copy(gnxt_ref, q, r).start(priority=1)
                scatter_copy(dprev_ref, q, r).start(priority=r % 2)

        x = _load_token_rows(xg_ref, MOE_TE)
        xb = x.astype(BF16)
        meta = xg_ref[pl.ds(D_MODEL // LANES, MOE_TE, stride=ROW_TILES), :]
        w_lo = meta[:, 2:3]
        w_hi = meta[:, 3:4]

        def expert(wg_ref, wu_ref, wd_ref):
            issue_group()
            gate = jnp.dot(xb, wg_ref[0], preferred_element_type=F32)
            issue_group()
            up = jnp.dot(xb, wu_ref[0], preferred_element_type=F32)
            issue_group()
            hidden = (gate * jax.nn.sigmoid(gate) * up).astype(BF16)
            issue_group()
            return jnp.dot(hidden, wd_ref[0], preferred_element_type=F32)

        ffn = w_lo * expert(wg0_ref, wu0_ref, wd0_ref) + w_hi * expert(wg1_ref, wu1_ref, wd1_ref)
        _store_token_rows(y_ref, MOE_TE, _layer_norm(alpha * x + ffn, g_ref[...], b_ref[...]),
                          jnp.zeros((MOE_TE, LANES), F32))

        @pl.when(i == ntiles - 1)
        def _():
            looped_rows(lambda r: gather_copy(gnxt_ref, q, r).wait())
            looped_rows(lambda r: scatter_copy(dcur_ref, p, r).start())
            looped_rows(lambda r: scatter_copy(dprev_ref, q, r).wait())
            looped_rows(lambda r: scatter_copy(dcur_ref, p, r).wait())

    @pl.when(i < ntiles)
    def _():
        parity = lax.rem(i, 2)
        for p in range(2):
            pl.when(parity == p)(functools.partial(step, p))


def _moe(x1ext, tile_lo, tile_hi, ntiles, gsrc, sdst, wg, wu, wd, g, b, alpha, n_tokens):
    n_tiles_max = gsrc.shape[0] - 1
    wspec_in = lambda which: pl.BlockSpec(
        (1, D_MODEL, D_EXPERT), lambda i, tlo, thi, nt: ((tlo, thi)[which][i], 0, 0))
    wspec_out = lambda which: pl.BlockSpec(
        (1, D_EXPERT, D_MODEL), lambda i, tlo, thi, nt: ((tlo, thi)[which][i], 0, 0))
    vec = pl.BlockSpec((1, D_MODEL), lambda i, *_: (0, 0))
    rows = lambda shift: pl.BlockSpec((1, 1, MOE_TE), lambda i, *_: (i + shift, 0, 0), memory_space=pltpu.SMEM)
    grid_spec = pltpu.PrefetchScalarGridSpec(
        num_scalar_prefetch=3,
        grid=(n_tiles_max,),
        in_specs=[
            rows(0), rows(1),
            rows(0), rows(1),
            pl.BlockSpec(memory_space=pl.ANY),
            wspec_in(0), wspec_in(0), wspec_out(0), wspec_in(1), wspec_in(1), wspec_out(1),
            vec, vec,
        ],
        out_specs=pl.BlockSpec(memory_space=pl.ANY),
        scratch_shapes=[pltpu.VMEM((MOE_TE * ROW_TILES, LANES), F32)] * 4
                       + [pltpu.SemaphoreType.DMA((2,)), pltpu.SemaphoreType.DMA((2,))],
    )
    return pl.pallas_call(
        functools.partial(_moe_kernel, alpha=alpha, n_tokens=n_tokens),
        grid_spec=grid_spec,
        out_shape=jax.ShapeDtypeStruct(((n_tokens + 2 * MOE_TE) * ROW_TILES, LANES), F32),
        compiler_params=_cparams(("arbitrary",)),
    )(tile_lo, tile_hi, ntiles, gsrc, gsrc, sdst, sdst, x1ext, wg, wu, wd, wg, wu, wd, g, b)


def _class_experts():
    lo, hi = [], []
    for grp in range(N_GROUPS):
        for a in range(EXPERTS_PER_GROUP):
            for c in range(a + 1, EXPERTS_PER_GROUP):
                lo.append(grp * EXPERTS_PER_GROUP + a)
                hi.append(grp * EXPERTS_PER_GROUP + c)
    return np.asarray(lo, np.int32), np.asarray(hi, np.int32)


def _routing_tables(route, counts, n_tiles_max):
    t = route.shape[1]
    cls = route[0].astype(jnp.int32)
    rank = route[1].astype(jnp.int32)
    cnt = counts[0, :N_CLASSES].astype(jnp.int32)
    padded = ((cnt + MOE_TE - 1) // MOE_TE) * MOE_TE
    ends = jnp.cumsum(padded)
    offs = ends - padded
    dest = offs[cls] + rank
    pos = np.arange(n_tiles_max * MOE_TE)
    spare = jnp.asarray(t + ((pos // MOE_TE) % 2) * MOE_TE + pos % MOE_TE, dtype=jnp.int32)
    dst_rows = spare.at[dest].set(jnp.arange(t, dtype=jnp.int32), unique_indices=True)
    src_rows = jnp.where(dst_rows < t, dst_rows, 0)
    placeholder = jnp.asarray(t + MOE_TE + np.arange(MOE_TE), dtype=jnp.int32)
    sdst = jnp.concatenate([placeholder, dst_rows]).reshape(n_tiles_max + 1, 1, MOE_TE) * ROW_TILES
    gsrc = jnp.concatenate([src_rows, jnp.zeros((MOE_TE,), jnp.int32)]).reshape(n_tiles_max + 1, 1, MOE_TE) * ROW_TILES
    tile_start = jnp.arange(n_tiles_max, dtype=jnp.int32) * MOE_TE
    tile_cls = jnp.sum((ends[None, :] <= tile_start[:, None]).astype(jnp.int32), axis=1)
    tile_cls = jnp.minimum(tile_cls, N_CLASSES - 1)
    ntiles = (ends[-1] // MOE_TE).astype(jnp.int32).reshape(1)
    e_lo, e_hi = _class_experts()
    return jnp.asarray(e_lo)[tile_cls], jnp.asarray(e_hi)[tile_cls], ntiles, gsrc, sdst


def kernel(x, w_in, rel_bias, w_up_a, w_up_b, w_out, ln1_g, ln1_b, w_router, b_router,
           w_gate, w_up, w_down, ln2_g, ln2_b):
    batch, seq, d = x.shape
    depth = w_in.shape[0]
    assert d == D_MODEL and seq % ATT_TQ == 0 and seq >= ATT_WIN
    t = batch * seq
    assert t % PROJ_TM == 0 and t % MIX_TM == 0
    alpha = (2 * depth) ** 0.25
    n_tiles_max = t // MOE_TE + N_CLASSES

    scan_m = _sb_scan_matrix()
    tri = jnp.asarray(np.tril(np.ones((MIX_TM, MIX_TM), np.float32), -1), dtype=BF16)
    wr = jnp.zeros((D_MODEL, LANES), F32).at[:, :N_EXPERTS].set(w_router.astype(F32))
    wr = jnp.concatenate(_split_bf16(wr), axis=1)
    br = jnp.zeros((1, LANES), F32).at[0, :N_EXPERTS].set(b_router.astype(F32))

    bias_tiles = _attn_bias_tiles(rel_bias)

    x2d = jnp.pad(x.reshape(t, d // LANES, LANES), ((0, 0), (0, ROW_TILES - d // LANES), (0, 0)))
    x2d = x2d.reshape(t * ROW_TILES, LANES)
    for l in range(depth):
        proj = _inproj(x2d, w_in[l].astype(BF16), t)
        ya = _attn_a(proj, bias_tiles[l], batch, seq)
        yb = _attn_b(proj, scan_m, batch, seq)
        x1ext, counts, route = _mix(ya, yb, proj, x2d, w_up_a[l].astype(BF16), w_up_b[l].astype(BF16),
                                    w_out[l].astype(BF16), ln1_g[l].reshape(1, d), ln1_b[l].reshape(1, d),
                                    wr, br, tri, alpha, t)
        tile_lo, tile_hi, ntiles, gsrc, sdst = _routing_tables(route, counts, n_tiles_max)
        x2d = _moe(x1ext, tile_lo, tile_hi, ntiles, gsrc, sdst,
                   w_gate[l].astype(BF16), w_up[l].astype(BF16), w_down[l].astype(BF16),
                   ln2_g[l].reshape(1, d), ln2_b[l].reshape(1, d), alpha, t)
    return x2d.reshape(-1, ROW_TILES, LANES)[:t, :d // LANES].reshape(batch, seq, d)
```

```python
import functools

import numpy as np
import jax
import jax.numpy as jnp
from jax import lax
from jax.experimental import pallas as pl
from jax.experimental.pallas import tpu as pltpu

F32 = jnp.float32
BF16 = jnp.bfloat16

D_MODEL = 1024
HEAD_DIM = 64
N_HEADS = 8
WIDTH = N_HEADS * HEAD_DIM
CHUNK = 64
LEFT_CHUNKS = 8
MAX_REL = 128
N_EXPERTS = 16
N_GROUPS = 4
EXPERTS_PER_GROUP = 4
D_EXPERT = 512
LN_EPS = 1e-5
PROJ_COLS = 6 * WIDTH + 2 * D_MODEL
N_PAIRS = EXPERTS_PER_GROUP * (EXPERTS_PER_GROUP - 1) // 2
N_CLASSES = N_GROUPS * N_PAIRS

LANES = 128
HEADS_PER_LANE_BLOCK = LANES // HEAD_DIM
N_HEAD_PAIRS = N_HEADS // HEADS_PER_LANE_BLOCK
VMEM_LIMIT = 56 * 1024 * 1024

PROJ_TM = 1024
PROJ_TN = 512
ATT_TQ = 256
ATT_WIN = ATT_TQ + LEFT_CHUNKS * CHUNK
SB_Q = 256
SB_BLK = 128
SB_EXIT = -110.0
MIX_TM = 512
MOE_TE = 256
MOE_ROW_UNROLL = 8
META_W = LANES

NEG_BIG = -1e30

ROW_TILES = D_MODEL // LANES + 1


def _cparams(sem):
    return pltpu.CompilerParams(dimension_semantics=sem, vmem_limit_bytes=VMEM_LIMIT)


def _load_token_rows(ref, n_rows):
    return jnp.concatenate([ref[pl.ds(c, n_rows, stride=ROW_TILES), :] for c in range(D_MODEL // LANES)], axis=1)


def _store_token_rows(ref, n_rows, values, extra):
    for c in range(D_MODEL // LANES):
        ref[pl.ds(c, n_rows, stride=ROW_TILES), :] = values[:, c * LANES:(c + 1) * LANES]
    ref[pl.ds(D_MODEL // LANES, n_rows, stride=ROW_TILES), :] = extra


def _inproj_kernel(x_ref, w_ref, o_ref, *, scale):
    xb = x_ref[...].astype(BF16)
    n_chunks = PROJ_COLS // PROJ_TN
    q_chunks = (0, 3 * WIDTH // PROJ_TN)
    gate_start = 6 * WIDTH // PROJ_TN
    for c in range(n_chunks):
        cols = slice(c * PROJ_TN, (c + 1) * PROJ_TN)
        acc = jnp.dot(xb, w_ref[:, cols], preferred_element_type=F32)
        if c in q_chunks:
            acc = acc * scale
        elif c >= gate_start:
            acc = jax.nn.sigmoid(acc)
        o_ref[:, cols] = acc.astype(BF16)


def _inproj(x2d, w_bf16, t):
    assert PROJ_TN == WIDTH
    return pl.pallas_call(
        functools.partial(_inproj_kernel, scale=HEAD_DIM ** -0.5),
        grid=(t // PROJ_TM,),
        in_specs=[pl.BlockSpec((PROJ_TM, D_MODEL), lambda i: (i, 0)),
                  pl.BlockSpec((D_MODEL, PROJ_COLS), lambda i: (0, 0), pipeline_mode=pl.Buffered(1))],
        out_specs=pl.BlockSpec((PROJ_TM, PROJ_COLS), lambda i: (i, 0)),
        out_shape=jax.ShapeDtypeStruct((t, PROJ_COLS), BF16),
        compiler_params=_cparams(("parallel",)),
    )(x2d, w_bf16)


def _attn_bias_tiles(rel_bias):
    depth = rel_bias.shape[0]
    period = ATT_TQ + ATT_WIN
    m = np.arange(period)
    key_minus_query = np.where(m < ATT_WIN, m, m - period)
    cases = np.arange(3)[:, None] * ATT_TQ
    idx = np.clip(cases - key_minus_query[None, :], -MAX_REL, MAX_REL) + MAX_REL
    diag = jnp.take(rel_bias.astype(F32), jnp.asarray(idx.reshape(-1)), axis=-1)
    diag = diag.reshape(depth, N_HEADS, 3, period).transpose(0, 2, 1, 3).reshape(depth * 3 * N_HEADS, 1, period)

    qpos = cases[:, :, None] + np.arange(ATT_TQ)[None, :, None]
    kpos = np.arange(ATT_WIN)[None, None, :]
    qc, kc = qpos // CHUNK, kpos // CHUNK
    allowed = jnp.asarray((kc <= qc) & (kc >= qc - LEFT_CHUNKS), dtype=F32)

    def tile_kernel(diag_ref, allowed_ref, o_ref):
        rows = jnp.broadcast_to(diag_ref[0], (ATT_TQ, period))
        toeplitz = pltpu.roll(rows, 0, 1, stride=1, stride_axis=0)
        o_ref[0] = jnp.where(allowed_ref[0] > 0.0, toeplitz[:, :ATT_WIN], NEG_BIG)

    tiles = pl.pallas_call(
        tile_kernel,
        grid=(depth * 3 * N_HEADS,),
        in_specs=[pl.BlockSpec((1, 1, period), lambda g: (g, 0, 0)),
                  pl.BlockSpec((1, ATT_TQ, ATT_WIN), lambda g: ((g // N_HEADS) % 3, 0, 0))],
        out_specs=pl.BlockSpec((1, ATT_TQ, ATT_WIN), lambda g: (g, 0, 0)),
        out_shape=jax.ShapeDtypeStruct((depth * 3 * N_HEADS, ATT_TQ, ATT_WIN), F32),
        compiler_params=_cparams(("parallel",)),
    )(diag, allowed)
    return tiles.reshape(depth, 3, N_HEADS, ATT_TQ, ATT_WIN)


def _attn_a_kernel(q_ref, k_ref, v_ref, bias_ref, o_ref):
    qt = pl.program_id(1)
    start = pl.multiple_of(jnp.maximum(qt * ATT_TQ - LEFT_CHUNKS * CHUNK, 0), ATT_TQ)
    first_head = lax.broadcasted_iota(jnp.int32, (1, LANES), 1) < HEAD_DIM
    for hp in range(N_HEAD_PAIRS):
        lanes = slice(hp * LANES, (hp + 1) * LANES)
        q = q_ref[:, lanes]
        kw = k_ref[pl.ds(start, ATT_WIN), lanes]
        vw = v_ref[pl.ds(start, ATT_WIN), lanes]
        vw = jnp.concatenate([vw, jnp.ones_like(vw)], axis=1)
        outs = []
        for h in range(HEADS_PER_LANE_BLOCK):
            head_lanes = first_head if h == 0 else jnp.logical_not(first_head)
            qh = jnp.where(head_lanes, q, jnp.zeros_like(q))
            s = lax.dot_general(qh, kw, (((1,), (1,)), ((), ())), preferred_element_type=F32)
            s = s + bias_ref[0, hp * HEADS_PER_LANE_BLOCK + h]
            m = jnp.max(s, axis=-1, keepdims=True)
            p = jnp.exp(s - m)
            o = jnp.dot(p.astype(BF16), vw, preferred_element_type=F32)
            outs.append(o[:, :LANES] / o[:, LANES:])
        o_ref[:, lanes] = jnp.where(first_head, outs[0], outs[1]).astype(BF16)


def _attn_a(proj, bias_tiles, batch, seq):
    n_qt = seq // ATT_TQ
    q_col, k_col, v_col = 0, 1, 2
    last_case = bias_tiles.shape[0] - 1
    return pl.pallas_call(
        _attn_a_kernel,
        grid=(batch, n_qt),
        in_specs=[
            pl.BlockSpec((ATT_TQ, WIDTH), lambda b, qt: (b * n_qt + qt, q_col)),
            pl.BlockSpec((seq, WIDTH), lambda b, qt: (b, k_col)),
            pl.BlockSpec((seq, WIDTH), lambda b, qt: (b, v_col)),
            pl.BlockSpec((1, N_HEADS, ATT_TQ, ATT_WIN), lambda b, qt: (jnp.minimum(qt, last_case), 0, 0, 0)),
        ],
        out_specs=pl.BlockSpec((ATT_TQ, WIDTH), lambda b, qt: (b * n_qt + qt, 0)),
        out_shape=jax.ShapeDtypeStruct((batch * seq, WIDTH), BF16),
        compiler_params=_cparams(("parallel", "arbitrary")),
    )(proj, proj, proj, bias_tiles)


def _sb_scan_matrix():
    jp = np.arange(SB_BLK)[:, None]
    c = np.arange(2 * SB_BLK)[None, :]
    half = np.where(c < SB_BLK, jp > c, True)
    return jnp.asarray(np.concatenate([half, half], axis=0), dtype=BF16)


def _sb_kernel(q_ref, k_ref, v_ref, m_ref, o_ref, acc_ref, carry_ref):
    qb = pl.program_id(1)
    first_head = lax.broadcasted_iota(jnp.int32, (1, LANES), 1) < HEAD_DIM

    acc_ref[...] = jnp.zeros_like(acc_ref)
    carry_ref[...] = jnp.zeros_like(carry_ref)

    def split_bf16(x):
        hi = lax.bitcast_convert_type(lax.bitcast_convert_type(x, jnp.int32) & jnp.int32(-65536), F32)
        return hi.astype(BF16), (x - hi).astype(BF16)

    def run_blocks(blocks):
        worst = None
        for hp in range(N_HEAD_PAIRS):
            lanes = slice(hp * LANES, (hp + 1) * LANES)
            staged = []
            for kb, row0, causal in blocks:
                n_rows = SB_Q - row0
                koff = pl.multiple_of(kb * SB_BLK, SB_BLK)
                q = q_ref[row0:, lanes]
                k = k_ref[pl.ds(koff, SB_BLK), lanes]
                v = v_ref[pl.ds(koff, SB_BLK), lanes]
                zero = jnp.zeros_like(k)
                k2 = jnp.concatenate([jnp.where(first_head, k, zero), jnp.where(first_head, zero, k)], axis=0)
                v2 = jnp.concatenate([jnp.where(first_head, v, zero), jnp.where(first_head, zero, v)], axis=0)
                z = lax.dot_general(q, k2, (((1,), (1,)), ((), ())), preferred_element_type=F32)
                log_beta = jnp.minimum(z, 0.0) - jnp.log(1.0 + jnp.exp(-jnp.abs(z)))
                log_keep = log_beta - z
                before2 = None
                if causal:
                    row = lax.broadcasted_iota(jnp.int32, (n_rows, 2 * SB_BLK), 0)
                    col = lax.broadcasted_iota(jnp.int32, (n_rows, 2 * SB_BLK), 1) & (SB_BLK - 1)
                    before2 = col < row
                    log_keep = jnp.where(before2, log_keep, 0.0)
                scans = []
                for h in range(HEADS_PER_LANE_BLOCK):
                    hi, lo = split_bf16(log_keep[:, h * SB_BLK:(h + 1) * SB_BLK])
                    scans.append(jnp.dot(jnp.concatenate([hi, lo], axis=1), m_ref[...], preferred_element_type=F32))
                staged.append((row0, log_beta, scans, before2, v2))

            carries = [carry_ref[hp * HEADS_PER_LANE_BLOCK + h] for h in range(HEADS_PER_LANE_BLOCK)]
            for row0, log_beta, scans, before2, v2 in staged:
                a_heads = []
                for h in range(HEADS_PER_LANE_BLOCK):
                    carry = carries[h]
                    a_heads.append(jnp.exp(log_beta[:, h * SB_BLK:(h + 1) * SB_BLK] + scans[h][:, :SB_BLK] + carry[row0:]))
                    moved = carry[row0:] + scans[h][:, SB_BLK:]
                    carries[h] = moved if row0 == 0 else jnp.concatenate([carry[:row0], moved], axis=0)
                a = jnp.concatenate(a_heads, axis=1)
                if before2 is not None:
                    a = jnp.where(before2, a, 0.0)
                acc_ref[hp, row0:, :] += jnp.dot(a.astype(BF16), v2, preferred_element_type=F32)
            for h in range(HEADS_PER_LANE_BLOCK):
                carry_ref[hp * HEADS_PER_LANE_BLOCK + h] = carries[h]
                worst = carries[h] if worst is None else jnp.maximum(worst, carries[h])
        return jnp.max(worst)

    blocks_per_tile = SB_Q // SB_BLK
    first_kb = qb * blocks_per_tile
    worst0 = run_blocks([(first_kb + d, d * SB_BLK, True) for d in reversed(range(blocks_per_tile))])

    def more(kb_min):
        return lambda state: jnp.logical_and(state[0] >= kb_min, state[1] >= SB_EXIT)

    def pair(state):
        kb, _ = state
        return kb - 2, run_blocks([(kb, 0, False), (kb - 1, 0, False)])

    def single(state):
        kb, _ = state
        return kb - 1, run_blocks([(kb, 0, False)])

    state = lax.while_loop(more(1), pair, (first_kb - 1, worst0))
    lax.while_loop(more(0), single, state)

    for hp in range(N_HEAD_PAIRS):
        o_ref[:, hp * LANES:(hp + 1) * LANES] = acc_ref[hp].astype(BF16)


def _attn_b(proj, scan_m, batch, seq):
    n_qb = seq // SB_Q
    q_col, k_col, v_col = 3, 4, 5
    return pl.pallas_call(
        _sb_kernel,
        grid=(batch, n_qb),
        in_specs=[
            pl.BlockSpec((SB_Q, WIDTH), lambda b, qb: (b * n_qb + qb, q_col)),
            pl.BlockSpec((seq, WIDTH), lambda b, qb: (b, k_col)),
            pl.BlockSpec((seq, WIDTH), lambda b, qb: (b, v_col)),
            pl.BlockSpec((2 * SB_BLK, 2 * SB_BLK), lambda b, qb: (0, 0)),
        ],
        out_specs=pl.BlockSpec((SB_Q, WIDTH), lambda b, qb: (b * n_qb + qb, 0)),
        out_shape=jax.ShapeDtypeStruct((batch * seq, WIDTH), BF16),
        scratch_shapes=[pltpu.VMEM((N_HEAD_PAIRS, SB_Q, LANES), F32),
                        pltpu.VMEM((N_HEADS, SB_Q, SB_BLK), F32)],
        compiler_params=_cparams(("parallel", "arbitrary")),
    )(proj, proj, proj, scan_m)


def _layer_norm(h, g, b):
    mu = jnp.mean(h, axis=-1, keepdims=True)
    d = h - mu
    var = jnp.mean(d * d, axis=-1, keepdims=True)
    return d * lax.rsqrt(var + LN_EPS) * g + b


def _split_bf16(x):
    hi = x.astype(BF16)
    return hi, (x - hi.astype(F32)).astype(BF16)


def _mix_kernel(ya_ref, yb_ref, ga_ref, gb_ref, x_ref, wa_ref, wb_ref, wo_ref, g_ref, b_ref,
                wr_ref, br_ref, tri_ref, o_ref, cnt_ref, route_ref, run_ref, *, alpha):
    @pl.when(pl.program_id(0) == 0)
    def _():
        run_ref[...] = jnp.zeros_like(run_ref)

    pa = jnp.dot(ya_ref[...], wa_ref[...], preferred_element_type=F32)
    pb = jnp.dot(yb_ref[...], wb_ref[...], preferred_element_type=F32)
    merged = ga_ref[...].astype(F32) * pa + gb_ref[...].astype(F32) * pb
    mixed = jnp.dot(merged.astype(BF16), wo_ref[...], preferred_element_type=F32)
    x1 = _layer_norm(alpha * x_ref[...] + mixed, g_ref[...], b_ref[...])

    xh, xl = _split_bf16(x1)
    rh = jnp.dot(xh, wr_ref[...], preferred_element_type=F32)
    rl = jnp.dot(xl, wr_ref[...], preferred_element_type=F32)
    logits = (rh[:, :LANES] + rl[:, :LANES]) + (rh[:, LANES:] + rl[:, LANES:]) + br_ref[...]
    lane = lax.broadcasted_iota(jnp.int32, logits.shape, 1)
    logits = jnp.where(lane < N_EXPERTS, logits, NEG_BIG)
    e = jnp.exp(logits - jnp.max(logits, axis=-1, keepdims=True))
    probs = e / jnp.sum(e, axis=-1, keepdims=True)
    p1 = jnp.max(probs, axis=-1, keepdims=True)
    i1 = jnp.min(jnp.where(probs == p1, lane, LANES), axis=-1, keepdims=True)
    group = i1 >> 2
    rest = jnp.where(jnp.logical_and(lane >> 2 == group, lane != i1), probs, -1.0)
    p2 = jnp.max(rest, axis=-1, keepdims=True)
    i2 = jnp.min(jnp.where(rest == p2, lane, LANES), axis=-1, keepdims=True)
    w1 = p1 / (p1 + p2)
    w2 = p2 / (p1 + p2)
    first_low = i1 < i2
    e_lo = jnp.where(first_low, i1, i2) & 3
    e_hi = jnp.where(first_low, i2, i1) & 3
    w_lo = jnp.where(first_low, w1, w2)
    w_hi = jnp.where(first_low, w2, w1)
    cls = group * N_PAIRS + ((e_lo * (5 - e_lo)) >> 1) + e_hi - 1

    onehot = (lane == cls).astype(F32)
    within = jnp.dot(tri_ref[...], onehot.astype(BF16), preferred_element_type=F32)
    rank = jnp.sum((within + run_ref[...]) * onehot, axis=-1, keepdims=True)
    run_ref[...] = run_ref[...] + jnp.sum(onehot, axis=0, keepdims=True)
    cnt_ref[...] = jnp.broadcast_to(run_ref[...], cnt_ref.shape)

    meta = jnp.where(lane == 0, cls.astype(F32),
                     jnp.where(lane == 1, rank,
                               jnp.where(lane == 2, w_lo, jnp.where(lane == 3, w_hi, 0.0))))
    _store_token_rows(o_ref, MIX_TM, x1, meta)
    route_ref[...] = meta.T[:8]


def _mix(ya, yb, proj, x2d, wa, wb, wo, g, b, wr, br, tri, alpha, t):
    ga_col = 6 * WIDTH // D_MODEL
    const = lambda shape: pl.BlockSpec(shape, lambda i: (0,) * len(shape))
    return pl.pallas_call(
        functools.partial(_mix_kernel, alpha=alpha),
        grid=(t // MIX_TM,),
        in_specs=[
            pl.BlockSpec((MIX_TM, WIDTH), lambda i: (i, 0)),
            pl.BlockSpec((MIX_TM, WIDTH), lambda i: (i, 0)),
            pl.BlockSpec((MIX_TM, D_MODEL), lambda i: (i, ga_col)),
            pl.BlockSpec((MIX_TM, D_MODEL), lambda i: (i, ga_col + 1)),
            pl.BlockSpec((MIX_TM, D_MODEL), lambda i: (i, 0)),
            const((WIDTH, D_MODEL)), const((WIDTH, D_MODEL)), const((D_MODEL, D_MODEL)),
            const((1, D_MODEL)), const((1, D_MODEL)),
            const((D_MODEL, 2 * LANES)), const((1, LANES)),
            const((MIX_TM, MIX_TM)),
        ],
        out_specs=[pl.BlockSpec((MIX_TM * ROW_TILES, LANES), lambda i: (i, 0)),
                   pl.BlockSpec((8, LANES), lambda i: (0, 0)),
                   pl.BlockSpec((8, MIX_TM), lambda i: (0, i))],
        out_shape=[jax.ShapeDtypeStruct((t * ROW_TILES, LANES), F32),
                   jax.ShapeDtypeStruct((8, LANES), F32),
                   jax.ShapeDtypeStruct((8, t), F32)],
        scratch_shapes=[pltpu.VMEM((1, LANES), F32)],
        compiler_params=_cparams(("arbitrary",)),
    )(ya, yb, proj, proj, x2d, wa, wb, wo, g, b, wr, br, tri)


def _moe_kernel(tlo_ref, thi_ref, ntiles_ref,
                gcur_ref, gnxt_ref, dprev_ref, dcur_ref, x_hbm,
                wg0_ref, wu0_ref, wd0_ref, wg1_ref, wu1_ref, wd1_ref, g_ref, b_ref,
                out_hbm, xg0_ref, xg1_ref, y0_ref, y1_ref, sem_in, sem_out, *, alpha, n_tokens):
    i = pl.program_id(0)
    ntiles = ntiles_ref[0]
    xg_refs = (xg0_ref, xg1_ref)
    y_refs = (y0_ref, y1_ref)

    def looped_rows(fn):
        def body(c, carry):
            base = pl.multiple_of(c * MOE_ROW_UNROLL, MOE_ROW_UNROLL)
            for u in range(MOE_ROW_UNROLL):
                fn(base + u)
            return carry
        lax.fori_loop(0, MOE_TE // MOE_ROW_UNROLL, body, 0)

    def gather_copy(idx_ref, s, r):
        return pltpu.make_async_copy(x_hbm.at[pl.ds(idx_ref[0, 0, r], ROW_TILES)],
                                     xg_refs[s].at[pl.ds(r * ROW_TILES, ROW_TILES)], sem_in.at[s])

    def scatter_copy(idx_ref, s, r):
        return pltpu.make_async_copy(y_refs[s].at[pl.ds(r, 1)], out_hbm.at[pl.ds(idx_ref[0, 0, r], 1)], sem_out.at[s])

    @pl.when(i == 0)
    def _():
        y1_ref[...] = jnp.zeros_like(y1_ref)
        fill = pltpu.make_async_copy(y1_ref, out_hbm.at[pl.ds(n_tokens, MOE_TE)], sem_out.at[1])
        fill.start()
        fill.wait()
        looped_rows(lambda r: gather_copy(gcur_ref, 0, r).start())

    def step(p):
        q = 1 - p
        xg_ref, y_ref = xg_refs[p], y_refs[p]
        looped_rows(lambda r: gather_copy(gcur_ref, p, r).wait())

        @pl.when(i >= 1)
        def _():
            looped_rows(lambda r: scatter_copy(dcur_ref, p, r).wait())

        n_groups = 8
        group_rows = MOE_TE // n_groups
        groups = iter(range(n_groups))

        def issue_group():
            base = next(groups) * group_rows
            for r in range(base, base + group_rows):
                gather_copy(gnxt_ref, q, r).start(priority=r % 2)
                scatter_copy(dprev_ref, q, r).start(priority=(r + 1) % 2)

        x = _load_token_rows(xg_ref, MOE_TE)
        xb = x.astype(BF16)
        meta = xg_ref[pl.ds(D_MODEL // LANES, MOE_TE, stride=ROW_TILES), :]
        w_lo = meta[:, 2:3]
        w_hi = meta[:, 3:4]

        def expert(wg_ref, wu_ref, wd_ref):
            issue_group()
            gate = jnp.dot(xb, wg_ref[0], preferred_element_type=F32)
            issue_group()
            up = jnp.dot(xb, wu_ref[0], preferred_element_type=F32)
            issue_group()
            hidden = (gate * jax.nn.sigmoid(gate) * up).astype(BF16)
            issue_group()
            return jnp.dot(hidden, wd_ref[0], preferred_element_type=F32)

        ffn = w_lo * expert(wg0_ref, wu0_ref, wd0_ref) + w_hi * expert(wg1_ref, wu1_ref, wd1_ref)
        y_ref[...] = _layer_norm(alpha * x + ffn, g_ref[...], b_ref[...])

        @pl.when(i == ntiles - 1)
        def _():
            looped_rows(lambda r: gather_copy(gnxt_ref, q, r).wait())
            looped_rows(lambda r: scatter_copy(dcur_ref, p, r).start())
            looped_rows(lambda r: scatter_copy(dprev_ref, q, r).wait())
            looped_rows(lambda r: scatter_copy(dcur_ref, p, r).wait())

    @pl.when(i < ntiles)
    def _():
        parity = lax.rem(i, 2)
        for p in range(2):
            pl.when(parity == p)(functools.partial(step, p))


def _moe(x1ext, tile_lo, tile_hi, ntiles, gsrc, sdst, wg, wu, wd, g, b, alpha, n_tokens):
    n_tiles_max = gsrc.shape[0] - 1
    wspec_in = lambda which: pl.BlockSpec(
        (1, D_MODEL, D_EXPERT), lambda i, tlo, thi, nt: ((tlo, thi)[which][i], 0, 0))
    wspec_out = lambda which: pl.BlockSpec(
        (1, D_EXPERT, D_MODEL), lambda i, tlo, thi, nt: ((tlo, thi)[which][i], 0, 0))
    vec = pl.BlockSpec((1, D_MODEL), lambda i, *_: (0, 0))
    rows = lambda shift: pl.BlockSpec((1, 1, MOE_TE), lambda i, *_: (i + shift, 0, 0), memory_space=pltpu.SMEM)
    grid_spec = pltpu.PrefetchScalarGridSpec(
        num_scalar_prefetch=3,
        grid=(n_tiles_max,),
        in_specs=[
            rows(0), rows(1),
            rows(0), rows(1),
            pl.BlockSpec(memory_space=pl.ANY),
            wspec_in(0), wspec_in(0), wspec_out(0), wspec_in(1), wspec_in(1), wspec_out(1),
            vec, vec,
        ],
        out_specs=pl.BlockSpec(memory_space=pl.ANY),
        scratch_shapes=[pltpu.VMEM((MOE_TE * ROW_TILES, LANES), F32), pltpu.VMEM((MOE_TE * ROW_TILES, LANES), F32),
                        pltpu.VMEM((MOE_TE, D_MODEL), F32), pltpu.VMEM((MOE_TE, D_MODEL), F32),
                        pltpu.SemaphoreType.DMA((2,)), pltpu.SemaphoreType.DMA((2,))],
    )
    return pl.pallas_call(
        functools.partial(_moe_kernel, alpha=alpha, n_tokens=n_tokens),
        grid_spec=grid_spec,
        out_shape=jax.ShapeDtypeStruct((n_tokens + 2 * MOE_TE, D_MODEL), F32),
        compiler_params=_cparams(("arbitrary",)),
    )(tile_lo, tile_hi, ntiles, gsrc, gsrc, sdst, sdst, x1ext, wg, wu, wd, wg, wu, wd, g, b)


def _class_experts():
    lo, hi = [], []
    for grp in range(N_GROUPS):
        for a in range(EXPERTS_PER_GROUP):
            for c in range(a + 1, EXPERTS_PER_GROUP):
                lo.append(grp * EXPERTS_PER_GROUP + a)
                hi.append(grp * EXPERTS_PER_GROUP + c)
    return np.asarray(lo, np.int32), np.asarray(hi, np.int32)


def _routing_tables(route, counts, n_tiles_max):
    t = route.shape[1]
    cls = route[0].astype(jnp.int32)
    rank = route[1].astype(jnp.int32)
    cnt = counts[0, :N_CLASSES].astype(jnp.int32)
    padded = ((cnt + MOE_TE - 1) // MOE_TE) * MOE_TE
    ends = jnp.cumsum(padded)
    offs = ends - padded
    dest = offs[cls] + rank
    pos = np.arange(n_tiles_max * MOE_TE)
    spare = jnp.asarray(t + ((pos // MOE_TE) % 2) * MOE_TE + pos % MOE_TE, dtype=jnp.int32)
    dst_rows = spare.at[dest].set(jnp.arange(t, dtype=jnp.int32), unique_indices=True)
    src_rows = jnp.where(dst_rows < t, dst_rows, 0)
    placeholder = jnp.asarray(t + MOE_TE + np.arange(MOE_TE), dtype=jnp.int32)
    sdst = jnp.concatenate([placeholder, dst_rows]).reshape(n_tiles_max + 1, 1, MOE_TE)
    gsrc = jnp.concatenate([src_rows, jnp.zeros((MOE_TE,), jnp.int32)]).reshape(n_tiles_max + 1, 1, MOE_TE) * ROW_TILES
    tile_start = jnp.arange(n_tiles_max, dtype=jnp.int32) * MOE_TE
    tile_cls = jnp.sum((ends[None, :] <= tile_start[:, None]).astype(jnp.int32), axis=1)
    tile_cls = jnp.minimum(tile_cls, N_CLASSES - 1)
    ntiles = (ends[-1] // MOE_TE).astype(jnp.int32).reshape(1)
    e_lo, e_hi = _class_experts()
    return jnp.asarray(e_lo)[tile_cls], jnp.asarray(e_hi)[tile_cls], ntiles, gsrc, sdst


def kernel(x, w_in, rel_bias, w_up_a, w_up_b, w_out, ln1_g, ln1_b, w_router, b_router,
           w_gate, w_up, w_down, ln2_g, ln2_b):
    batch, seq, d = x.shape
    depth = w_in.shape[0]
    assert d == D_MODEL and seq % ATT_TQ == 0 and seq >= ATT_WIN
    t = batch * seq
    assert t % PROJ_TM == 0 and t % MIX_TM == 0
    alpha = (2 * depth) ** 0.25
    n_tiles_max = t // MOE_TE + N_CLASSES

    scan_m = _sb_scan_matrix()
    tri = jnp.asarray(np.tril(np.ones((MIX_TM, MIX_TM), np.float32), -1), dtype=BF16)
    wr = jnp.zeros((D_MODEL, LANES), F32).at[:, :N_EXPERTS].set(w_router.astype(F32))
    wr = jnp.concatenate(_split_bf16(wr), axis=1)
    br = jnp.zeros((1, LANES), F32).at[0, :N_EXPERTS].set(b_router.astype(F32))

    bias_tiles = _attn_bias_tiles(rel_bias)

    x2d = x.reshape(t, d)
    for l in range(depth):
        proj = _inproj(x2d, w_in[l].astype(BF16), t)
        ya = _attn_a(proj, bias_tiles[l], batch, seq)
        yb = _attn_b(proj, scan_m, batch, seq)
        x1ext, counts, route = _mix(ya, yb, proj, x2d, w_up_a[l].astype(BF16), w_up_b[l].astype(BF16),
                                    w_out[l].astype(BF16), ln1_g[l].reshape(1, d), ln1_b[l].reshape(1, d),
                                    wr, br, tri, alpha, t)
        tile_lo, tile_hi, ntiles, gsrc, sdst = _routing_tables(route, counts, n_tiles_max)
        x2d = _moe(x1ext, tile_lo, tile_hi, ntiles, gsrc, sdst,
                   w_gate[l].astype(BF16), w_up[l].astype(BF16), w_down[l].astype(BF16),
                   ln2_g[l].reshape(1, d), ln2_b[l].reshape(1, d), alpha, t)
    return x2d[:t].reshape(batch, seq, d)
```

```python
import functools

import numpy as np
import jax
import jax.numpy as jnp
from jax import lax
from jax.experimental import pallas as pl
from jax.experimental.pallas import tpu as pltpu

F32 = jnp.float32
BF16 = jnp.bfloat16

D_MODEL = 1024
HEAD_DIM = 64
N_HEADS = 8
WIDTH = N_HEADS * HEAD_DIM
CHUNK = 64
LEFT_CHUNKS = 8
MAX_REL = 128
N_EXPERTS = 16
N_GROUPS = 4
EXPERTS_PER_GROUP = 4
D_EXPERT = 512
LN_EPS = 1e-5
PROJ_COLS = 6 * WIDTH + 2 * D_MODEL
N_PAIRS = EXPERTS_PER_GROUP * (EXPERTS_PER_GROUP - 1) // 2
N_CLASSES = N_GROUPS * N_PAIRS

LANES = 128
HEADS_PER_LANE_BLOCK = LANES // HEAD_DIM
N_HEAD_PAIRS = N_HEADS // HEADS_PER_LANE_BLOCK
VMEM_LIMIT = 56 * 1024 * 1024

PROJ_TM = 1024
PROJ_TN = 512
ATT_TQ = 256
ATT_WIN = ATT_TQ + LEFT_CHUNKS * CHUNK
SB_Q = 256
SB_BLK = 128
SB_EXIT = -110.0
MIX_TM = 1024
MOE_TE = 256
MOE_ROW_UNROLL = 8
META_W = LANES

NEG_BIG = -1e30

ROW_TILES = D_MODEL // LANES + 1


def _cparams(sem):
    return pltpu.CompilerParams(dimension_semantics=sem, vmem_limit_bytes=VMEM_LIMIT)


def _load_token_rows(ref, n_rows):
    return jnp.concatenate([ref[pl.ds(c, n_rows, stride=ROW_TILES), :] for c in range(D_MODEL // LANES)], axis=1)


def _store_token_rows(ref, n_rows, values, extra):
    for c in range(D_MODEL // LANES):
        ref[pl.ds(c, n_rows, stride=ROW_TILES), :] = values[:, c * LANES:(c + 1) * LANES]
    ref[pl.ds(D_MODEL // LANES, n_rows, stride=ROW_TILES), :] = extra


def _inproj_kernel(x_ref, w_ref, o_ref, *, scale):
    xb = x_ref[...].astype(BF16)
    n_chunks = PROJ_COLS // PROJ_TN
    q_chunks = (0, 3 * WIDTH // PROJ_TN)
    gate_start = 6 * WIDTH // PROJ_TN
    for c in range(n_chunks):
        cols = slice(c * PROJ_TN, (c + 1) * PROJ_TN)
        acc = jnp.dot(xb, w_ref[:, cols], preferred_element_type=F32)
        if c in q_chunks:
            acc = acc * scale
        elif c >= gate_start:
            acc = jax.nn.sigmoid(acc)
        o_ref[:, cols] = acc.astype(BF16)


def _inproj(x2d, w_bf16, t):
    assert PROJ_TN == WIDTH
    return pl.pallas_call(
        functools.partial(_inproj_kernel, scale=HEAD_DIM ** -0.5),
        grid=(t // PROJ_TM,),
        in_specs=[pl.BlockSpec((PROJ_TM, D_MODEL), lambda i: (i, 0)),
                  pl.BlockSpec((D_MODEL, PROJ_COLS), lambda i: (0, 0), pipeline_mode=pl.Buffered(1))],
        out_specs=pl.BlockSpec((PROJ_TM, PROJ_COLS), lambda i: (i, 0)),
        out_shape=jax.ShapeDtypeStruct((t, PROJ_COLS), BF16),
        compiler_params=_cparams(("parallel",)),
    )(x2d, w_bf16)


def _attn_bias_tiles(rel_bias):
    depth = rel_bias.shape[0]
    period = ATT_TQ + ATT_WIN
    m = np.arange(period)
    key_minus_query = np.where(m < ATT_WIN, m, m - period)
    cases = np.arange(3)[:, None] * ATT_TQ
    idx = np.clip(cases - key_minus_query[None, :], -MAX_REL, MAX_REL) + MAX_REL
    diag = jnp.take(rel_bias.astype(F32), jnp.asarray(idx.reshape(-1)), axis=-1)
    diag = diag.reshape(depth, N_HEADS, 3, period).transpose(0, 2, 1, 3).reshape(depth * 3 * N_HEADS, 1, period)

    qpos = cases[:, :, None] + np.arange(ATT_TQ)[None, :, None]
    kpos = np.arange(ATT_WIN)[None, None, :]
    qc, kc = qpos // CHUNK, kpos // CHUNK
    allowed = jnp.asarray((kc <= qc) & (kc >= qc - LEFT_CHUNKS), dtype=F32)

    def tile_kernel(diag_ref, allowed_ref, o_ref):
        rows = jnp.broadcast_to(diag_ref[0], (ATT_TQ, period))
        toeplitz = pltpu.roll(rows, 0, 1, stride=1, stride_axis=0)
        o_ref[0] = jnp.where(allowed_ref[0] > 0.0, toeplitz[:, :ATT_WIN], NEG_BIG)

    tiles = pl.pallas_call(
        tile_kernel,
        grid=(depth * 3 * N_HEADS,),
        in_specs=[pl.BlockSpec((1, 1, period), lambda g: (g, 0, 0)),
                  pl.BlockSpec((1, ATT_TQ, ATT_WIN), lambda g: ((g // N_HEADS) % 3, 0, 0))],
        out_specs=pl.BlockSpec((1, ATT_TQ, ATT_WIN), lambda g: (g, 0, 0)),
        out_shape=jax.ShapeDtypeStruct((depth * 3 * N_HEADS, ATT_TQ, ATT_WIN), F32),
        compiler_params=_cparams(("parallel",)),
    )(diag, allowed)
    return tiles.reshape(depth, 3, N_HEADS, ATT_TQ, ATT_WIN)


def _attn_a_kernel(q_ref, k_ref, v_ref, bias_ref, o_ref):
    qt = pl.program_id(1)
    start = pl.multiple_of(jnp.maximum(qt * ATT_TQ - LEFT_CHUNKS * CHUNK, 0), ATT_TQ)
    first_head = lax.broadcasted_iota(jnp.int32, (1, LANES), 1) < HEAD_DIM
    for hp in range(N_HEAD_PAIRS):
        lanes = slice(hp * LANES, (hp + 1) * LANES)
        q = q_ref[:, lanes]
        kw = k_ref[pl.ds(start, ATT_WIN), lanes]
        vw = v_ref[pl.ds(start, ATT_WIN), lanes]
        vw = jnp.concatenate([vw, jnp.ones_like(vw)], axis=1)
        outs = []
        for h in range(HEADS_PER_LANE_BLOCK):
            head_lanes = first_head if h == 0 else jnp.logical_not(first_head)
            qh = jnp.where(head_lanes, q, jnp.zeros_like(q))
            s = lax.dot_general(qh, kw, (((1,), (1,)), ((), ())), preferred_element_type=F32)
            s = s + bias_ref[0, hp * HEADS_PER_LANE_BLOCK + h]
            m = jnp.max(s, axis=-1, keepdims=True)
            p = jnp.exp(s - m)
            o = jnp.dot(p.astype(BF16), vw, preferred_element_type=F32)
            outs.append(o[:, :LANES] / o[:, LANES:])
        o_ref[:, lanes] = jnp.where(first_head, outs[0], outs[1]).astype(BF16)


def _attn_a(proj, bias_tiles, batch, seq):
    n_qt = seq // ATT_TQ
    q_col, k_col, v_col = 0, 1, 2
    last_case = bias_tiles.shape[0] - 1
    return pl.pallas_call(
        _attn_a_kernel,
        grid=(batch, n_qt),
        in_specs=[
            pl.BlockSpec((ATT_TQ, WIDTH), lambda b, qt: (b * n_qt + qt, q_col)),
            pl.BlockSpec((seq, WIDTH), lambda b, qt: (b, k_col)),
            pl.BlockSpec((seq, WIDTH), lambda b, qt: (b, v_col)),
            pl.BlockSpec((1, N_HEADS, ATT_TQ, ATT_WIN), lambda b, qt: (jnp.minimum(qt, last_case), 0, 0, 0)),
        ],
        out_specs=pl.BlockSpec((ATT_TQ, WIDTH), lambda b, qt: (b * n_qt + qt, 0)),
        out_shape=jax.ShapeDtypeStruct((batch * seq, WIDTH), BF16),
        compiler_params=_cparams(("parallel", "arbitrary")),
    )(proj, proj, proj, bias_tiles)


def _sb_scan_matrix():
    jp = np.arange(SB_BLK)[:, None]
    c = np.arange(2 * SB_BLK)[None, :]
    half = np.where(c < SB_BLK, jp > c, True)
    return jnp.asarray(np.concatenate([half, half], axis=0), dtype=BF16)


def _sb_kernel(q_ref, k_ref, v_ref, m_ref, o_ref, acc_ref, carry_ref):
    qb = pl.program_id(1)
    first_head = lax.broadcasted_iota(jnp.int32, (1, LANES), 1) < HEAD_DIM

    acc_ref[...] = jnp.zeros_like(acc_ref)
    carry_ref[...] = jnp.zeros_like(carry_ref)

    def split_bf16(x):
        hi = lax.bitcast_convert_type(lax.bitcast_convert_type(x, jnp.int32) & jnp.int32(-65536), F32)
        return hi.astype(BF16), (x - hi).astype(BF16)

    def run_blocks(blocks):
        worst = None
        for hp in range(N_HEAD_PAIRS):
            lanes = slice(hp * LANES, (hp + 1) * LANES)
            staged = []
            for kb, row0, causal in blocks:
                n_rows = SB_Q - row0
                koff = pl.multiple_of(kb * SB_BLK, SB_BLK)
                q = q_ref[row0:, lanes]
                k = k_ref[pl.ds(koff, SB_BLK), lanes]
                v = v_ref[pl.ds(koff, SB_BLK), lanes]
                zero = jnp.zeros_like(k)
                k2 = jnp.concatenate([jnp.where(first_head, k, zero), jnp.where(first_head, zero, k)], axis=0)
                v2 = jnp.concatenate([jnp.where(first_head, v, zero), jnp.where(first_head, zero, v)], axis=0)
                z = lax.dot_general(q, k2, (((1,), (1,)), ((), ())), preferred_element_type=F32)
                log_beta = jnp.minimum(z, 0.0) - jnp.log(1.0 + jnp.exp(-jnp.abs(z)))
                log_keep = log_beta - z
                before2 = None
                if causal:
                    row = lax.broadcasted_iota(jnp.int32, (n_rows, 2 * SB_BLK), 0)
                    col = lax.broadcasted_iota(jnp.int32, (n_rows, 2 * SB_BLK), 1) & (SB_BLK - 1)
                    before2 = col < row
                    log_keep = jnp.where(before2, log_keep, 0.0)
                scans = []
                for h in range(HEADS_PER_LANE_BLOCK):
                    hi, lo = split_bf16(log_keep[:, h * SB_BLK:(h + 1) * SB_BLK])
                    scans.append(jnp.dot(jnp.concatenate([hi, lo], axis=1), m_ref[...], preferred_element_type=F32))
                staged.append((row0, log_beta, scans, before2, v2))

            carries = [carry_ref[hp * HEADS_PER_LANE_BLOCK + h] for h in range(HEADS_PER_LANE_BLOCK)]
            for row0, log_beta, scans, before2, v2 in staged:
                a_heads = []
                for h in range(HEADS_PER_LANE_BLOCK):
                    carry = carries[h]
                    a_heads.append(jnp.exp(log_beta[:, h * SB_BLK:(h + 1) * SB_BLK] + scans[h][:, :SB_BLK] + carry[row0:]))
                    moved = carry[row0:] + scans[h][:, SB_BLK:]
                    carries[h] = moved if row0 == 0 else jnp.concatenate([carry[:row0], moved], axis=0)
                a = jnp.concatenate(a_heads, axis=1)
                if before2 is not None:
                    a = jnp.where(before2, a, 0.0)
                acc_ref[hp, row0:, :] += jnp.dot(a.astype(BF16), v2, preferred_element_type=F32)
            for h in range(HEADS_PER_LANE_BLOCK):
                carry_ref[hp * HEADS_PER_LANE_BLOCK + h] = carries[h]
                worst = carries[h] if worst is None else jnp.maximum(worst, carries[h])
        return jnp.max(worst)

    blocks_per_tile = SB_Q // SB_BLK
    first_kb = qb * blocks_per_tile
    worst0 = run_blocks([(first_kb + d, d * SB_BLK, True) for d in reversed(range(blocks_per_tile))])

    def more(kb_min):
        return lambda state: jnp.logical_and(state[0] >= kb_min, state[1] >= SB_EXIT)

    def pair(state):
        kb, _ = state
        return kb - 2, run_blocks([(kb, 0, False), (kb - 1, 0, False)])

    def single(state):
        kb, _ = state
        return kb - 1, run_blocks([(kb, 0, False)])

    first_pair = lambda state: jnp.logical_and(more(1)(state), state[0] == first_kb - 1)
    state = lax.while_loop(first_pair, pair, (first_kb - 1, worst0))
    lax.while_loop(more(0), single, state)

    for hp in range(N_HEAD_PAIRS):
        o_ref[:, hp * LANES:(hp + 1) * LANES] = acc_ref[hp].astype(BF16)


def _attn_b(proj, scan_m, batch, seq):
    n_qb = seq // SB_Q
    q_col, k_col, v_col = 3, 4, 5
    return pl.pallas_call(
        _sb_kernel,
        grid=(batch, n_qb),
        in_specs=[
            pl.BlockSpec((SB_Q, WIDTH), lambda b, qb: (b * n_qb + qb, q_col)),
            pl.BlockSpec((seq, WIDTH), lambda b, qb: (b, k_col)),
            pl.BlockSpec((seq, WIDTH), lambda b, qb: (b, v_col)),
            pl.BlockSpec((2 * SB_BLK, 2 * SB_BLK), lambda b, qb: (0, 0)),
        ],
        out_specs=pl.BlockSpec((SB_Q, WIDTH), lambda b, qb: (b * n_qb + qb, 0)),
        out_shape=jax.ShapeDtypeStruct((batch * seq, WIDTH), BF16),
        scratch_shapes=[pltpu.VMEM((N_HEAD_PAIRS, SB_Q, LANES), F32),
                        pltpu.VMEM((N_HEADS, SB_Q, SB_BLK), F32)],
        compiler_params=_cparams(("parallel", "arbitrary")),
    )(proj, proj, proj, scan_m)


def _layer_norm(h, g, b):
    mu = jnp.mean(h, axis=-1, keepdims=True)
    d = h - mu
    var = jnp.mean(d * d, axis=-1, keepdims=True)
    return d * lax.rsqrt(var + LN_EPS) * g + b


def _split_bf16(x):
    hi = x.astype(BF16)
    return hi, (x - hi.astype(F32)).astype(BF16)


def _mix_kernel(ya_ref, yb_ref, ga_ref, gb_ref, x_ref, wa_ref, wb_ref, wo_ref, g_ref, b_ref,
                wr_ref, br_ref, tri_ref, o_ref, cnt_ref, route_ref, run_ref, *, alpha):
    @pl.when(pl.program_id(0) == 0)
    def _():
        run_ref[...] = jnp.zeros_like(run_ref)

    pa = jnp.dot(ya_ref[...], wa_ref[...], preferred_element_type=F32)
    pb = jnp.dot(yb_ref[...], wb_ref[...], preferred_element_type=F32)
    merged = ga_ref[...].astype(F32) * pa + gb_ref[...].astype(F32) * pb
    mixed = jnp.dot(merged.astype(BF16), wo_ref[...], preferred_element_type=F32)
    x1 = _layer_norm(alpha * x_ref[...] + mixed, g_ref[...], b_ref[...])

    xh, xl = _split_bf16(x1)
    rh = jnp.dot(xh, wr_ref[...], preferred_element_type=F32)
    rl = jnp.dot(xl, wr_ref[...], preferred_element_type=F32)
    logits = (rh[:, :LANES] + rl[:, :LANES]) + (rh[:, LANES:] + rl[:, LANES:]) + br_ref[...]
    lane = lax.broadcasted_iota(jnp.int32, logits.shape, 1)
    logits = jnp.where(lane < N_EXPERTS, logits, NEG_BIG)
    e = jnp.exp(logits - jnp.max(logits, axis=-1, keepdims=True))
    probs = e / jnp.sum(e, axis=-1, keepdims=True)
    p1 = jnp.max(probs, axis=-1, keepdims=True)
    i1 = jnp.min(jnp.where(probs == p1, lane, LANES), axis=-1, keepdims=True)
    group = i1 >> 2
    rest = jnp.where(jnp.logical_and(lane >> 2 == group, lane != i1), probs, -1.0)
    p2 = jnp.max(rest, axis=-1, keepdims=True)
    i2 = jnp.min(jnp.where(rest == p2, lane, LANES), axis=-1, keepdims=True)
    w1 = p1 / (p1 + p2)
    w2 = p2 / (p1 + p2)
    first_low = i1 < i2
    e_lo = jnp.where(first_low, i1, i2) & 3
    e_hi = jnp.where(first_low, i2, i1) & 3
    w_lo = jnp.where(first_low, w1, w2)
    w_hi = jnp.where(first_low, w2, w1)
    cls = group * N_PAIRS + ((e_lo * (5 - e_lo)) >> 1) + e_hi - 1

    onehot = (lane == cls).astype(F32)
    within = jnp.dot(tri_ref[...], onehot.astype(BF16), preferred_element_type=F32)
    rank = jnp.sum((within + run_ref[...]) * onehot, axis=-1, keepdims=True)
    run_ref[...] = run_ref[...] + jnp.sum(onehot, axis=0, keepdims=True)
    cnt_ref[...] = jnp.broadcast_to(run_ref[...], cnt_ref.shape)

    meta = jnp.where(lane == 0, cls.astype(F32),
                     jnp.where(lane == 1, rank,
                               jnp.where(lane == 2, w_lo, jnp.where(lane == 3, w_hi, 0.0))))
    _store_token_rows(o_ref, MIX_TM, x1, meta)
    route_ref[...] = meta.T[:8]


def _mix(ya, yb, proj, x2d, wa, wb, wo, g, b, wr, br, tri, alpha, t):
    ga_col = 6 * WIDTH // D_MODEL
    const = lambda shape: pl.BlockSpec(shape, lambda i: (0,) * len(shape))
    return pl.pallas_call(
        functools.partial(_mix_kernel, alpha=alpha),
        grid=(t // MIX_TM,),
        in_specs=[
            pl.BlockSpec((MIX_TM, WIDTH), lambda i: (i, 0)),
            pl.BlockSpec((MIX_TM, WIDTH), lambda i: (i, 0)),
            pl.BlockSpec((MIX_TM, D_MODEL), lambda i: (i, ga_col)),
            pl.BlockSpec((MIX_TM, D_MODEL), lambda i: (i, ga_col + 1)),
            pl.BlockSpec((MIX_TM, D_MODEL), lambda i: (i, 0)),
            const((WIDTH, D_MODEL)), const((WIDTH, D_MODEL)), const((D_MODEL, D_MODEL)),
            const((1, D_MODEL)), const((1, D_MODEL)),
            const((D_MODEL, 2 * LANES)), const((1, LANES)),
            const((MIX_TM, MIX_TM)),
        ],
        out_specs=[pl.BlockSpec((MIX_TM * ROW_TILES, LANES), lambda i: (i, 0)),
                   pl.BlockSpec((8, LANES), lambda i: (0, 0)),
                   pl.BlockSpec((8, MIX_TM), lambda i: (0, i))],
        out_shape=[jax.ShapeDtypeStruct((t * ROW_TILES, LANES), F32),
                   jax.ShapeDtypeStruct((8, LANES), F32),
                   jax.ShapeDtypeStruct((8, t), F32)],
        scratch_shapes=[pltpu.VMEM((1, LANES), F32)],
        compiler_params=_cparams(("arbitrary",)),
    )(ya, yb, proj, proj, x2d, wa, wb, wo, g, b, wr, br, tri)


def _moe_kernel(tlo_ref, thi_ref, ntiles_ref,
                gcur_ref, gnxt_ref, dprev_ref, dcur_ref, x_hbm,
                wg0_ref, wu0_ref, wd0_ref, wg1_ref, wu1_ref, wd1_ref, g_ref, b_ref,
                out_hbm, xg0_ref, xg1_ref, y0_ref, y1_ref, sem_in, sem_out, *, alpha, n_tokens):
    i = pl.program_id(0)
    ntiles = ntiles_ref[0]
    xg_refs = (xg0_ref, xg1_ref)
    y_refs = (y0_ref, y1_ref)

    def looped_rows(fn):
        def body(c, carry):
            base = pl.multiple_of(c * MOE_ROW_UNROLL, MOE_ROW_UNROLL)
            for u in range(MOE_ROW_UNROLL):
                fn(base + u)
            return carry
        lax.fori_loop(0, MOE_TE // MOE_ROW_UNROLL, body, 0)

    def gather_copy(idx_ref, s, r):
        return pltpu.make_async_copy(x_hbm.at[pl.ds(idx_ref[0, 0, r], ROW_TILES)],
                                     xg_refs[s].at[pl.ds(r * ROW_TILES, ROW_TILES)], sem_in.at[s])

    def scatter_copy(idx_ref, s, r):
        return pltpu.make_async_copy(y_refs[s].at[pl.ds(r, 1)], out_hbm.at[pl.ds(idx_ref[0, 0, r], 1)], sem_out.at[s])

    @pl.when(i == 0)
    def _():
        y1_ref[...] = jnp.zeros_like(y1_ref)
        fill = pltpu.make_async_copy(y1_ref, out_hbm.at[pl.ds(n_tokens, MOE_TE)], sem_out.at[1])
        fill.start()
        fill.wait()
        looped_rows(lambda r: gather_copy(gcur_ref, 0, r).start())

    def step(p):
        q = 1 - p
        xg_ref, y_ref = xg_refs[p], y_refs[p]
        looped_rows(lambda r: gather_copy(gcur_ref, p, r).wait())

        @pl.when(i >= 1)
        def _():
            looped_rows(lambda r: scatter_copy(dcur_ref, p, r).wait())

        n_groups = 8
        group_rows = MOE_TE // n_groups
        groups = iter(range(n_groups))

        def issue_group():
            base = next(groups) * group_rows
            for r in range(base, base + group_rows):
                gather_copy(gnxt_ref, q, r).start(priority=r % 2)
                scatter_copy(dprev_ref, q, r).start(priority=(r + 1) % 2)

        x = _load_token_rows(xg_ref, MOE_TE)
        xb = x.astype(BF16)
        meta = xg_ref[pl.ds(D_MODEL // LANES, MOE_TE, stride=ROW_TILES), :]
        w_lo = meta[:, 2:3]
        w_hi = meta[:, 3:4]

        def expert(wg_ref, wu_ref, wd_ref):
            issue_group()
            gate = jnp.dot(xb, wg_ref[0], preferred_element_type=F32)
            issue_group()
            up = jnp.dot(xb, wu_ref[0], preferred_element_type=F32)
            issue_group()
            hidden = (gate * jax.nn.sigmoid(gate) * up).astype(BF16)
            issue_group()
            return jnp.dot(hidden, wd_ref[0], preferred_element_type=F32)

        ffn = w_lo * expert(wg0_ref, wu0_ref, wd0_ref) + w_hi * expert(wg1_ref, wu1_ref, wd1_ref)
        y_ref[...] = _layer_norm(alpha * x + ffn, g_ref[...], b_ref[...])

        @pl.when(i == ntiles - 1)
        def _():
            looped_rows(lambda r: gather_copy(gnxt_ref, q, r).wait())
            looped_rows(lambda r: scatter_copy(dcur_ref, p, r).start())
            looped_rows(lambda r: scatter_copy(dprev_ref, q, r).wait())
            looped_rows(lambda r: scatter_copy(dcur_ref, p, r).wait())

    @pl.when(i < ntiles)
    def _():
        parity = lax.rem(i, 2)
        for p in range(2):
            pl.when(parity == p)(functools.partial(step, p))


def _moe(x1ext, tile_lo, tile_hi, ntiles, gsrc, sdst, wg, wu, wd, g, b, alpha, n_tokens):
    n_tiles_max = gsrc.shape[0] - 1
    wspec_in = lambda which: pl.BlockSpec(
        (1, D_MODEL, D_EXPERT), lambda i, tlo, thi, nt: ((tlo, thi)[which][i], 0, 0))
    wspec_out = lambda which: pl.BlockSpec(
        (1, D_EXPERT, D_MODEL), lambda i, tlo, thi, nt: ((tlo, thi)[which][i], 0, 0))
    vec = pl.BlockSpec((1, D_MODEL), lambda i, *_: (0, 0))
    rows = lambda shift: pl.BlockSpec((1, 1, MOE_TE), lambda i, *_: (i + shift, 0, 0), memory_space=pltpu.SMEM)
    grid_spec = pltpu.PrefetchScalarGridSpec(
        num_scalar_prefetch=3,
        grid=(n_tiles_max,),
        in_specs=[
            rows(0), rows(1),
            rows(0), rows(1),
            pl.BlockSpec(memory_space=pl.ANY),
            wspec_in(0), wspec_in(0), wspec_out(0), wspec_in(1), wspec_in(1), wspec_out(1),
            vec, vec,
        ],
        out_specs=pl.BlockSpec(memory_space=pl.ANY),
        scratch_shapes=[pltpu.VMEM((MOE_TE * ROW_TILES, LANES), F32), pltpu.VMEM((MOE_TE * ROW_TILES, LANES), F32),
                        pltpu.VMEM((MOE_TE, D_MODEL), F32), pltpu.VMEM((MOE_TE, D_MODEL), F32),
                        pltpu.SemaphoreType.DMA((2,)), pltpu.SemaphoreType.DMA((2,))],
    )
    return pl.pallas_call(
        functools.partial(_moe_kernel, alpha=alpha, n_tokens=n_tokens),
        grid_spec=grid_spec,
        out_shape=jax.ShapeDtypeStruct((n_tokens + 2 * MOE_TE, D_MODEL), F32),
        compiler_params=_cparams(("arbitrary",)),
    )(tile_lo, tile_hi, ntiles, gsrc, gsrc, sdst, sdst, x1ext, wg, wu, wd, wg, wu, wd, g, b)


def _class_experts():
    lo, hi = [], []
    for grp in range(N_GROUPS):
        for a in range(EXPERTS_PER_GROUP):
            for c in range(a + 1, EXPERTS_PER_GROUP):
                lo.append(grp * EXPERTS_PER_GROUP + a)
                hi.append(grp * EXPERTS_PER_GROUP + c)
    return np.asarray(lo, np.int32), np.asarray(hi, np.int32)


def _routing_tables(route, counts, n_tiles_max):
    t = route.shape[1]
    cls = route[0].astype(jnp.int32)
    rank = route[1].astype(jnp.int32)
    cnt = counts[0, :N_CLASSES].astype(jnp.int32)
    padded = ((cnt + MOE_TE - 1) // MOE_TE) * MOE_TE
    ends = jnp.cumsum(padded)
    offs = ends - padded
    dest = offs[cls] + rank
    pos = np.arange(n_tiles_max * MOE_TE)
    spare = jnp.asarray(t + ((pos // MOE_TE) % 2) * MOE_TE + pos % MOE_TE, dtype=jnp.int32)
    dst_rows = spare.at[dest].set(jnp.arange(t, dtype=jnp.int32), unique_indices=True)
    src_rows = jnp.where(dst_rows < t, dst_rows, 0)
    placeholder = jnp.asarray(t + MOE_TE + np.arange(MOE_TE), dtype=jnp.int32)
    sdst = jnp.concatenate([placeholder, dst_rows]).reshape(n_tiles_max + 1, 1, MOE_TE)
    gsrc = jnp.concatenate([src_rows, jnp.zeros((MOE_TE,), jnp.int32)]).reshape(n_tiles_max + 1, 1, MOE_TE) * ROW_TILES
    tile_start = jnp.arange(n_tiles_max, dtype=jnp.int32) * MOE_TE
    tile_cls = jnp.sum((ends[None, :] <= tile_start[:, None]).astype(jnp.int32), axis=1)
    tile_cls = jnp.minimum(tile_cls, N_CLASSES - 1)
    ntiles = (ends[-1] // MOE_TE).astype(jnp.int32).reshape(1)
    e_lo, e_hi = _class_experts()
    return jnp.asarray(e_lo)[tile_cls], jnp.asarray(e_hi)[tile_cls], ntiles, gsrc, sdst


def kernel(x, w_in, rel_bias, w_up_a, w_up_b, w_out, ln1_g, ln1_b, w_router, b_router,
           w_gate, w_up, w_down, ln2_g, ln2_b):
    batch, seq, d = x.shape
    depth = w_in.shape[0]
    assert d == D_MODEL and seq % ATT_TQ == 0 and seq >= ATT_WIN
    t = batch * seq
    assert t % PROJ_TM == 0 and t % MIX_TM == 0
    alpha = (2 * depth) ** 0.25
    n_tiles_max = t // MOE_TE + N_CLASSES

    scan_m = _sb_scan_matrix()
    tri = jnp.asarray(np.tril(np.ones((MIX_TM, MIX_TM), np.float32), -1), dtype=BF16)
    wr = jnp.zeros((D_MODEL, LANES), F32).at[:, :N_EXPERTS].set(w_router.astype(F32))
    wr = jnp.concatenate(_split_bf16(wr), axis=1)
    br = jnp.zeros((1, LANES), F32).at[0, :N_EXPERTS].set(b_router.astype(F32))

    bias_tiles = _attn_bias_tiles(rel_bias)

    x2d = x.reshape(t, d)
    for l in range(depth):
        proj = _inproj(x2d, w_in[l].astype(BF16), t)
        ya = _attn_a(proj, bias_tiles[l], batch, seq)
        yb = _attn_b(proj, scan_m, batch, seq)
        x1ext, counts, route = _mix(ya, yb, proj, x2d, w_up_a[l].astype(BF16), w_up_b[l].astype(BF16),
                                    w_out[l].astype(BF16), ln1_g[l].reshape(1, d), ln1_b[l].reshape(1, d),
                                    wr, br, tri, alpha, t)
        tile_lo, tile_hi, ntiles, gsrc, sdst = _routing_tables(route, counts, n_tiles_max)
        x2d = _moe(x1ext, tile_lo, tile_hi, ntiles, gsrc, sdst,
                   w_gate[l].astype(BF16), w_up[l].astype(BF16), w_down[l].astype(BF16),
                   ln2_g[l].reshape(1, d), ln2_b[l].reshape(1, d), alpha, t)
    return x2d[:t].reshape(batch, seq, d)
```
